```python
import math
import jax, jax.numpy as jnp
from jax import lax
import numpy as np

D_MODEL = 2048
BATCH = 2
SEQ = 8192
DEPTH = 2

GRID_W = 64
CTX_LEN = 256

GM_HEADS = 4
GM_HEAD_DIM = 128
GM_WIDTH = GM_HEADS * GM_HEAD_DIM
GM_CHUNK = 128
SSM_HEADS = 8
SSM_HEAD_DIM = 64
SSM_INNER = SSM_HEADS * SSM_HEAD_DIM
SSM_GROUPS = 2
SSM_STATE = 128
SSM_CONV = 5
SSM_CHUNK = 128
DT_MIN = 1e-3
DT_MAX = 1e-1
MLA_HEADS = 8
QK_NOPE = 128
QK_ROPE = 64
V_DIM = 128
Q_LORA = 768
KV_LORA = 256
MLA_WIDTH = MLA_HEADS * V_DIM
ROPE_BASE = 10000.0
ATTN_BLOCK = 128

MIX_WIDTH = GM_WIDTH + SSM_INNER + MLA_WIDTH

N_EXPERTS = 64
TOP_K = 8
N_EXPERT_GROUPS = 8
TOPK_GROUPS = 4
EXPERT_FF = 512
SHARED_FF = 512
ROUTED_SCALE = 2.5
EXPERT_BLOCK = 256

DN_ALPHA = (2 * DEPTH) ** 0.25
DN_BETA = (8 * DEPTH) ** -0.25
LN_EPS = 1e-5
RMS_EPS = 1e-6

XBC_WIDTH = SSM_INNER + 2 * SSM_GROUPS * SSM_STATE
OFF_GM = 0
OFF_Q = OFF_GM + 2 * GM_WIDTH
OFF_Z = OFF_Q + Q_LORA
OFF_XBC = OFF_Z + SSM_INNER
OFF_DT = OFF_XBC + XBC_WIDTH
OFF_KV = OFF_DT + 2 * SSM_HEADS
OFF_KR = OFF_KV + KV_LORA
IN_WIDTH = OFF_KR + QK_ROPE
T_DT = OFF_DT - OFF_XBC
T_KV = OFF_KV - OFF_XBC
T_KR = OFF_KR - OFF_XBC

F32 = jnp.float32

kernel_name = 'hybrid_gmlp_ssd_mla_moe_diffusion_block'


def layer_norm(x, g, b):
    xf = x.astype(F32)
    mu = jnp.mean(xf, -1, keepdims=True)
    var = jnp.mean(jnp.square(xf - mu), -1, keepdims=True)
    return ((xf - mu) * lax.rsqrt(var + LN_EPS) * g.astype(F32) + b.astype(F32)).astype(x.dtype)


def rms_norm(x, g):
    xf = x.astype(F32)
    return (xf * lax.rsqrt(jnp.mean(jnp.square(xf), -1, keepdims=True) + RMS_EPS) * g.astype(F32)).astype(x.dtype)


def gated_group_rms_norm(y, z, g):
    yz = (y * jax.nn.silu(z)).astype(F32)
    grp = yz.reshape(*yz.shape[:-1], SSM_GROUPS, SSM_INNER // SSM_GROUPS)
    grp = grp * lax.rsqrt(jnp.mean(jnp.square(grp), -1, keepdims=True) + RMS_EPS)
    return (grp.reshape(yz.shape) * g.astype(F32)).astype(y.dtype)


def axial_rope_tables(n_tokens):
    rows = n_tokens // GRID_W
    row = jnp.repeat(jnp.arange(rows), GRID_W)
    col = jnp.tile(jnp.arange(GRID_W), rows)
    half = QK_ROPE // 2
    inv = 1.0 / (ROPE_BASE ** (jnp.arange(0, half, 2, dtype=F32) / half))
    ang = jnp.stack([row[:, None] * inv, col[:, None] * inv], axis=1)
    return jnp.cos(ang), jnp.sin(ang)


def apply_axial_rope(x, cos, sin):
    xs = x.astype(F32).reshape(*x.shape[:-1], 2, 2, QK_ROPE // 4)
    x1, x2 = xs[..., 0, :], xs[..., 1, :]
    out = jnp.stack([x1 * cos - x2 * sin, x2 * cos + x1 * sin], axis=-2)
    return out.reshape(x.shape).astype(x.dtype)


def split_tail(t):
    return t[..., :T_DT], t[..., T_DT:T_KV], t[..., T_KV:T_KR], t[..., T_KR:]


def chunk_gmlp(uv, p):
    uv = jax.nn.gelu(uv, approximate=False)
    u, v = uv[..., :GM_WIDTH], uv[..., GM_WIDTH:]
    v = layer_norm(v, p['g_v'], p['b_v'])
    b, l, _ = v.shape
    vc = v.reshape(b, l // GM_CHUNK, GM_CHUNK, GM_HEADS, GM_HEAD_DIM)
    s = jnp.einsum('hts,bcshd->bcthd', p['w_sp'], vc) + p['b_sp'].T[:, :, None]
    return u * s.reshape(b, l, GM_WIDTH)


def centred_conv_silu(u, w, bias):
    y = lax.conv_general_dilated(u, w[:, None, :].astype(u.dtype), window_strides=(1,),
                                 padding=[(SSM_CONV // 2, SSM_CONV // 2)],
                                 dimension_numbers=('NWC', 'WIO', 'NWC'),
                                 feature_group_count=u.shape[-1])
    return jax.nn.silu(y + bias)


def ssm_inputs(xbc, dt_raw, p):
    u = centred_conv_silu(xbc, p['conv_w'], p['conv_b'])
    lead = u.shape[:-1]
    gs = SSM_GROUPS * SSM_STATE
    rep = SSM_HEADS // SSM_GROUPS
    xh = u[..., :SSM_INNER].reshape(*lead, SSM_HEADS, SSM_HEAD_DIM)
    bm = jnp.repeat(u[..., SSM_INNER:SSM_INNER + gs].reshape(*lead, SSM_GROUPS, SSM_STATE), rep, axis=-2)
    cm = jnp.repeat(u[..., SSM_INNER + gs:].reshape(*lead, SSM_GROUPS, SSM_STATE), rep, axis=-2)
    dt = jax.nn.softplus(dt_raw.astype(F32).reshape(*lead, 2, SSM_HEADS) + p['dt_bias'].astype(F32))
    return xh, bm, cm, dt


def ssd_scan(xh, dt, a, bm, cm, h0, need_y):
    b, l, h, pd = xh.shape
    n = bm.shape[-1]
    nc, q = l // SSM_CHUNK, SSM_CHUNK
    xdt = (xh.astype(F32) * dt[..., None]).reshape(b, nc, q, h, pd)
    bc = bm.astype(F32).reshape(b, nc, q, h, n)
    acs = jnp.cumsum(jnp.moveaxis((dt * a).reshape(b, nc, q, h), 3, 1), axis=-1)
    states = jnp.einsum('bclhn,bhcl,bclhp->bchpn', bc, jnp.exp(acs[..., -1:] - acs), xdt)
    chunk_decay = jnp.exp(acs[..., -1])

    def step(carry, inp):
        st, dec = inp
        return carry * dec[..., None, None] + st, carry

    h_final, h_prev = lax.scan(step, h0.astype(F32),
                               (jnp.moveaxis(states, 1, 0), jnp.moveaxis(chunk_decay, 2, 0)))
    if not need_y:
        return None, h_final
    h_prev = jnp.moveaxis(h_prev, 0, 1)
    cc = cm.astype(F32).reshape(b, nc, q, h, n)
    seg = acs[..., :, None] - acs[..., None, :]
    mask = jnp.tril(jnp.ones((q, q), dtype=bool))
    lmat = jnp.exp(jnp.where(mask, seg, -jnp.inf))
    y = (jnp.einsum('bclhn,bcshn,bhcls,bcshp->bclhp', cc, bc, lmat, xdt)
         + jnp.einsum('bclhn,bchpn,bhcl->bclhp', cc, h_prev, jnp.exp(acs)))
    return y.reshape(b, l, h, pd).astype(xh.dtype), h_final


def bidirectional_ssd(lat, ctx_in, a_log, d_skip, ctx_out):
    a = -jnp.exp(a_log.astype(F32))
    xl, bl, cl, dl = lat
    xc, bc, cc, dc = ctx_in
    h0 = jnp.zeros((xl.shape[0], SSM_HEADS, SSM_HEAD_DIM, SSM_STATE), F32)
    rev = lambda t: jnp.flip(t, axis=1)
    yc_f, hc_f = ssd_scan(xc, dc[..., 0, :], a[0], bc, cc, h0, ctx_out)
    yc_b, hc_b = ssd_scan(rev(xc), rev(dc[..., 1, :]), a[1], rev(bc), rev(cc), h0, ctx_out)
    yl_f, _ = ssd_scan(xl, dl[..., 0, :], a[0], bl, cl, hc_f, True)
    yl_b, _ = ssd_scan(rev(xl), rev(dl[..., 1, :]), a[1], rev(bl), rev(cl), hc_b, True)
    skip = d_skip[:, None]
    y_lat = yl_f + rev(yl_b) + skip * xl
    y_ctx = (yc_f + rev(yc_b) + skip * xc) if ctx_out else None
    return y_lat, y_ctx


def mla_keys_values(ckv, kr, p, rope):
    kv = rms_norm(ckv, p['g_kv']) @ p['w_kv_b']
    kv = kv.reshape(*kv.shape[:-1], MLA_HEADS, QK_NOPE + V_DIM)
    k_nope, v = kv[..., :QK_NOPE], kv[..., QK_NOPE:]
    if rope is not None:
        kr = apply_axial_rope(kr, rope[0], rope[1])
    k = jnp.concatenate([k_nope, jnp.broadcast_to(kr[..., None, :], k_nope.shape[:-1] + (QK_ROPE,))], -1)
    return k, v


def mla_queries(cq, p, rope):
    q = rms_norm(cq, p['g_q']) @ p['w_q_b']
    q = q.reshape(*q.shape[:-1], MLA_HEADS, QK_NOPE + QK_ROPE)
    if rope is not None:
        cos, sin = rope
        q = jnp.concatenate([q[..., :QK_NOPE], apply_axial_rope(q[..., QK_NOPE:], cos[:, None], sin[:, None])], -1)
    return q


def block_attention(q, k, v):
    b, lq, h, dk = q.shape
    scale = dk ** -0.5
    qb = jnp.moveaxis(q.reshape(b, lq // ATTN_BLOCK, ATTN_BLOCK, h, dk), 1, 0)

    def one(qi):
        s = jnp.einsum('bqhd,bkhd->bhqk', qi, k).astype(F32) * scale
        pr = jax.nn.softmax(s, axis=-1).astype(v.dtype)
        return jnp.einsum('bhqk,bkhd->bqhd', pr, v)

    o = lax.map(one, qb)
    return jnp.moveaxis(o, 0, 1).reshape(b, lq, h * v.shape[-1])


def token_mixers(h, hc, p, rope, ctx_out):
    w_in = p['w_in']
    proj_l = h @ w_in
    tail_l = proj_l[..., OFF_XBC:]
    if ctx_out:
        proj_c = hc @ w_in
        tail_c = proj_c[..., OFF_XBC:]
    else:
        tail_c = hc @ w_in[:, OFF_XBC:]
    xbc_l, dt_l, ckv_l, kr_l = split_tail(tail_l)
    xbc_c, dt_c, ckv_c, kr_c = split_tail(tail_c)

    gm_l = chunk_gmlp(proj_l[..., OFF_GM:OFF_Q], p)

    y_l, y_c = bidirectional_ssd(ssm_inputs(xbc_l, dt_l, p), ssm_inputs(xbc_c, dt_c, p),
                                 p['a_log'], p['d_skip'], ctx_out)
    ssm_l = gated_group_rms_norm(y_l.reshape(*y_l.shape[:2], SSM_INNER), proj_l[..., OFF_Z:OFF_XBC], p['g_ssm'])

    k_l, v_l = mla_keys_values(ckv_l, kr_l, p, rope)
    k_c, v_c = mla_keys_values(ckv_c, kr_c, p, None)
    q_l = mla_queries(proj_l[..., OFF_Q:OFF_Z], p, rope)
    att_l = block_attention(q_l, jnp.concatenate([k_c, k_l], 1), jnp.concatenate([v_c, v_l], 1))

    mix_l = jnp.concatenate([gm_l, ssm_l, att_l], -1) @ p['w_out']
    if not ctx_out:
        return mix_l, None
    gm_c = chunk_gmlp(proj_c[..., OFF_GM:OFF_Q], p)
    ssm_c = gated_group_rms_norm(y_c.reshape(*y_c.shape[:2], SSM_INNER), proj_c[..., OFF_Z:OFF_XBC], p['g_ssm'])
    att_c = block_attention(mla_queries(proj_c[..., OFF_Q:OFF_Z], p, None), k_c, v_c)
    mix_c = jnp.concatenate([gm_c, ssm_c, att_c], -1) @ p['w_out']
    return mix_l, mix_c


def routed_experts(hf, idx, wts, w_g, w_u, w_d):
    t, k = idx.shape
    e = w_g.shape[0]
    n_assign = t * k
    flat_e = idx.reshape(-1)
    flat_tok = jnp.repeat(jnp.arange(t, dtype=jnp.int32), k)
    flat_w = wts.reshape(-1)
    order = jnp.argsort(flat_e, stable=True)
    e_sorted = flat_e[order]
    counts = jnp.bincount(flat_e, length=e)
    padded = (counts + EXPERT_BLOCK - 1) // EXPERT_BLOCK * EXPERT_BLOCK
    start = jnp.cumsum(counts) - counts
    pend = jnp.cumsum(padded)
    pstart = pend - padded
    dest = pstart[e_sorted] + jnp.arange(n_assign) - start[e_sorted]
    n_blocks = (n_assign + e * (EXPERT_BLOCK - 1) + EXPERT_BLOCK - 1) // EXPERT_BLOCK
    cap = n_blocks * EXPERT_BLOCK
    slot_tok = jnp.zeros((cap,), jnp.int32).at[dest].set(flat_tok[order])
    slot_w = jnp.zeros((cap,), hf.dtype).at[dest].set(flat_w[order])
    block_e = jnp.minimum(jnp.searchsorted(pend, jnp.arange(n_blocks) * EXPERT_BLOCK, side='right'), e - 1)

    def body(out, blk):
        tok, w, ex = blk
        xb = hf[tok]
        hb = jax.nn.silu(xb @ w_g[ex]) * (xb @ w_u[ex])
        return out.at[tok].add((hb @ w_d[ex]) * w[:, None]), None

    out, _ = lax.scan(body, jnp.zeros_like(hf),
                      (slot_tok.reshape(n_blocks, EXPERT_BLOCK), slot_w.reshape(n_blocks, EXPERT_BLOCK), block_e))
    return out


def moe_ffn(hf, p):
    t = hf.shape[0]
    scores = jax.nn.sigmoid((hf @ p['w_router']).astype(F32))
    biased = scores + p['b_router'].astype(F32)
    per = N_EXPERTS // N_EXPERT_GROUPS
    grp_score = jnp.sum(lax.top_k(biased.reshape(t, N_EXPERT_GROUPS, per), 2)[0], -1)
    _, gidx = lax.top_k(grp_score, TOPK_GROUPS)
    gmask = jnp.any(gidx[..., None] == jnp.arange(N_EXPERT_GROUPS), axis=1)
    emask = jnp.repeat(gmask, per, axis=1)
    _, eidx = lax.top_k(jnp.where(emask, biased, -jnp.inf), TOP_K)
    w = jnp.take_along_axis(scores, eidx, axis=1)
    w = (w / jnp.sum(w, -1, keepdims=True) * ROUTED_SCALE).astype(hf.dtype)
    routed = routed_experts(hf, eidx, w, p['w_e_gate'], p['w_e_up'], p['w_e_down'])
    shared = (jax.nn.silu(hf @ p['w_sh_gate']) * (hf @ p['w_sh_up'])) @ p['w_sh_down']
    return routed + shared


def trunk_layer(x, ctx, mod_l, mod_c, p, rope, last):
    b, n, d = x.shape
    h = x * (1 + mod_l[:, 1, None]) + mod_l[:, 0, None]
    hc = ctx * (1 + mod_c[1]) + mod_c[0]
    mix_l, mix_c = token_mixers(h, hc, p, rope, not last)
    x = layer_norm(DN_ALPHA * x + mod_l[:, 2, None] * mix_l, p['ln1_g'], p['ln1_b'])
    tokens = (x * (1 + mod_l[:, 4, None]) + mod_l[:, 3, None]).reshape(b * n, d)
    if not last:
        ctx = layer_norm(DN_ALPHA * ctx + mod_c[2] * mix_c, p['ln1_g'], p['ln1_b'])
        tokens_c = (ctx * (1 + mod_c[4]) + mod_c[3]).reshape(-1, d)
        tokens = jnp.concatenate([tokens, tokens_c], 0)
    f = moe_ffn(tokens, p)
    x = layer_norm(DN_ALPHA * x + mod_l[:, 5, None] * f[:b * n].reshape(b, n, d), p['ln2_g'], p['ln2_b'])
    if not last:
        ctx = layer_norm(DN_ALPHA * ctx + mod_c[5] * f[b * n:].reshape(ctx.shape), p['ln2_g'], p['ln2_b'])
    return x, ctx


def setup_inputs(seed: int = 0) -> dict:
    key = jax.random.key(seed)
    ks = iter(jax.random.split(key, 48))
    nrm = lambda shape, scale: jax.random.normal(next(ks), shape, F32) * scale
    L, D = DEPTH, D_MODEL
    dt0 = jnp.exp(jax.random.uniform(next(ks), (L, 2, SSM_HEADS), F32, math.log(DT_MIN), math.log(DT_MAX)))
    a_log = jnp.log(jax.random.uniform(next(ks), (L, 2, SSM_HEADS), F32, 1.0, 16.0))
    return {
        'x': nrm((BATCH, SEQ, D), 1.0),
        'c': nrm((BATCH, D), 1.0),
        'ctx': nrm((BATCH, CTX_LEN, D), 1.0),
        'c_ctx': nrm((D,), 1.0),
        'w_mod': nrm((L, D, 6 * D), 0.5 * D ** -0.5),
        'b_mod': nrm((L, 6 * D), 0.02),
        'w_in': nrm((L, D, IN_WIDTH), D ** -0.5),
        'g_q': 1.0 + nrm((L, Q_LORA), 0.02),
        'w_q_b': nrm((L, Q_LORA, MLA_HEADS * (QK_NOPE + QK_ROPE)), Q_LORA ** -0.5),
        'g_kv': 1.0 + nrm((L, KV_LORA), 0.02),
        'w_kv_b': nrm((L, KV_LORA, MLA_HEADS * (QK_NOPE + V_DIM)), KV_LORA ** -0.5),
        'conv_w': nrm((L, SSM_CONV, XBC_WIDTH), SSM_CONV ** -0.5),
        'conv_b': nrm((L, XBC_WIDTH), 0.02),
        'a_log': a_log,
        'dt_bias': dt0 + jnp.log(-jnp.expm1(-dt0)),
        'd_skip': 1.0 + nrm((L, SSM_HEADS), 0.02),
        'g_ssm': 1.0 + nrm((L, SSM_INNER), 0.02),
        'g_v': 1.0 + nrm((L, GM_WIDTH), 0.02),
        'b_v': nrm((L, GM_WIDTH), 0.02),
        'w_sp': nrm((L, GM_HEADS, GM_CHUNK, GM_CHUNK), GM_CHUNK ** -0.5),
        'b_sp': 1.0 + nrm((L, GM_HEADS, GM_CHUNK), 0.02),
        'w_out': nrm((L, MIX_WIDTH, D), DN_BETA * MIX_WIDTH ** -0.5),
        'ln1_g': 1.0 + nrm((L, D), 0.02),
        'ln1_b': nrm((L, D), 0.02),
        'ln2_g': 1.0 + nrm((L, D), 0.02),
        'ln2_b': nrm((L, D), 0.02),
        'w_router': nrm((L, D, N_EXPERTS), D ** -0.5),
        'b_router': nrm((L, N_EXPERTS), 0.01),
        'w_e_gate': nrm((L, N_EXPERTS, D, EXPERT_FF), D ** -0.5),
        'w_e_up': nrm((L, N_EXPERTS, D, EXPERT_FF), D ** -0.5),
        'w_e_down': nrm((L, N_EXPERTS, EXPERT_FF, D), DN_BETA * EXPERT_FF ** -0.5),
        'w_sh_gate': nrm((L, D, SHARED_FF), D ** -0.5),
        'w_sh_up': nrm((L, D, SHARED_FF), D ** -0.5),
        'w_sh_down': nrm((L, SHARED_FF, D), DN_BETA * SHARED_FF ** -0.5),
    }


def reference(x, c, ctx, c_ctx, w_mod, b_mod, w_in, g_q, w_q_b, g_kv, w_kv_b, conv_w, conv_b, a_log, dt_bias,
              d_skip, g_ssm, g_v, b_v, w_sp, b_sp, w_out, ln1_g, ln1_b, ln2_g, ln2_b, w_router, b_router,
              w_e_gate, w_e_up, w_e_down, w_sh_gate, w_sh_up, w_sh_down):
    rope = axial_rope_tables(x.shape[1])
    for l in range(DEPTH):
        last = l == DEPTH - 1
        n_mod_c = 2 if last else 6
        mod_l = (jax.nn.silu(c) @ w_mod[l] + b_mod[l]).reshape(c.shape[0], 6, D_MODEL)
        mod_c = (jax.nn.silu(c_ctx) @ w_mod[l][:, :n_mod_c * D_MODEL]
                 + b_mod[l][:n_mod_c * D_MODEL]).reshape(n_mod_c, D_MODEL)
        p = dict(w_in=w_in[l], g_q=g_q[l], w_q_b=w_q_b[l], g_kv=g_kv[l], w_kv_b=w_kv_b[l],
                 conv_w=conv_w[l], conv_b=conv_b[l], a_log=a_log[l], dt_bias=dt_bias[l], d_skip=d_skip[l],
                 g_ssm=g_ssm[l], g_v=g_v[l], b_v=b_v[l], w_sp=w_sp[l], b_sp=b_sp[l], w_out=w_out[l],
                 ln1_g=ln1_g[l], ln1_b=ln1_b[l], ln2_g=ln2_g[l], ln2_b=ln2_b[l],
                 w_router=w_router[l], b_router=b_router[l], w_e_gate=w_e_gate[l], w_e_up=w_e_up[l],
                 w_e_down=w_e_down[l], w_sh_gate=w_sh_gate[l], w_sh_up=w_sh_up[l], w_sh_down=w_sh_down[l])
        x, ctx = trunk_layer(x, ctx, mod_l, mod_c, p, rope, last)
    return x
```

```python
import functools
import math

import jax
import jax.numpy as jnp
import numpy as np
from jax import lax
from jax.experimental import pallas as pl
from jax.experimental.pallas import tpu as pltpu

F32 = jnp.float32
BF16 = jnp.bfloat16
HIGHEST = lax.Precision.HIGHEST

D_MODEL = 2048
GRID_W = 64
GM_HEADS = 4
GM_WIDTH = 512
GM_CHUNK = 128
SSM_HEADS = 8
SSM_HEAD_DIM = 64
SSM_INNER = 512
SSM_GROUPS = 2
SSM_STATE = 128
SSM_CONV = 5
SSM_CHUNK = 128
XBC_WIDTH = 1024
MLA_HEADS = 8
QK_NOPE = 128
QK_ROPE = 64
V_DIM = 128
Q_LORA = 768
KV_LORA = 256
ROPE_BASE = 10000.0
N_EXPERTS = 64
TOP_K = 8
N_EXPERT_GROUPS = 8
TOPK_GROUPS = 4
EXPERT_FF = 512
SHARED_FF = 512
ROUTED_SCALE = 2.5
LN_EPS = 1e-5
RMS_EPS = 1e-6
DEPTH = 2
DN_ALPHA = (2 * DEPTH) ** 0.25

R_GM = 0
R_Q = R_GM + 2 * GM_WIDTH
R_Z = R_Q + Q_LORA
R_XBC = R_Z + SSM_INNER
R_DT = R_XBC + XBC_WIDTH
R_KV = R_DT + 2 * SSM_HEADS
R_KR = R_KV + KV_LORA

LANE = 128
P_GM = 0
P_Q = P_GM + 2 * GM_WIDTH
P_Z = P_Q + Q_LORA
P_XBC = P_Z + SSM_INNER
P_DT = P_XBC + XBC_WIDTH
P_KV = P_DT + 2 * LANE
P_KR = P_KV + KV_LORA
P_WIDTH = P_KR + LANE

HEAD_PAD = 256
ROW_TILE = 256
EXPERT_BLOCK = 256
VMEM_LIMIT = 56 * 1024 * 1024


def _cparams(sem):
    return pltpu.CompilerParams(dimension_semantics=sem, vmem_limit_bytes=VMEM_LIMIT)


def _resident(shape):
    n = len(shape)
    return pl.BlockSpec(shape, lambda *_: (0,) * n, pipeline_mode=pl.Buffered(1))


def _silu(x):
    return x * jax.nn.sigmoid(x)


def _layer_norm(y, g, b):
    mu = jnp.mean(y, axis=-1, keepdims=True)
    yc = y - mu
    var = jnp.mean(yc * yc, axis=-1, keepdims=True)
    return yc * lax.rsqrt(var + LN_EPS) * g + b


def _rms_norm(y, g):
    return y * lax.rsqrt(jnp.mean(y * y, axis=-1, keepdims=True) + RMS_EPS) * g


def _mod_kernel(c_ref, w_ref, b_ref, o_ref):
    c = c_ref[...]
    o_ref[0] = jnp.dot(_silu(c), w_ref[0], precision=HIGHEST, preferred_element_type=F32) + b_ref[0]


def _modulation(cvec, w_mod, b_mod):
    depth, d, n = w_mod.shape
    tn = 1536
    return pl.pallas_call(
        _mod_kernel,
        grid=(depth, n // tn),
        in_specs=[
            pl.BlockSpec((8, d), lambda l, j: (0, 0)),
            pl.BlockSpec((1, d, tn), lambda l, j: (l, 0, j)),
            pl.BlockSpec((1, 1, tn), lambda l, j: (l, 0, j)),
        ],
        out_specs=pl.BlockSpec((1, 8, tn), lambda l, j: (l, 0, j)),
        out_shape=jax.ShapeDtypeStruct((depth, 8, n), F32),
        compiler_params=_cparams(("arbitrary", "arbitrary")),
        name="modulation",
    )(cvec, w_mod, b_mod.reshape(depth, 1, n))


def _proj_in_kernel(x_ref, mod_ref, ca_ref, cb_ref, w_ref, gv_ref, bv_ref, wsp_ref, bsp_ref,
                    gq_ref, wq_ref, gkv_ref, wkv_ref,
                    gm_ref, z_ref, xbc_ref, dt_ref, q_ref, k_ref, v_ref):
    tm = x_ref.shape[0]
    mod = mod_ref[0]
    h = (x_ref[...] * (1.0 + mod[1:2, :]) + mod[0:1, :]).astype(BF16)

    def proj(lo, n):
        return jnp.dot(h, w_ref[:, lo:lo + n], preferred_element_type=F32)

    uv = proj(P_GM, 2 * GM_WIDTH)
    gl = 0.5 * uv * (1.0 + lax.erf(uv * np.float32(math.sqrt(0.5))))
    u = gl[:, :GM_WIDTH]
    vn = _layer_norm(gl[:, GM_WIDTH:], gv_ref[...], bv_ref[...]).astype(BF16)
    for c in range(tm // GM_CHUNK):
        r0 = c * GM_CHUNK
        parts = [jnp.dot(wsp_ref[hh], vn[r0:r0 + GM_CHUNK, hh * LANE:(hh + 1) * LANE],
                         preferred_element_type=F32) for hh in range(GM_HEADS)]
        s = jnp.concatenate(parts, axis=1) + bsp_ref[...]
        gm_ref[r0:r0 + GM_CHUNK, :] = (u[r0:r0 + GM_CHUNK, :] * s).astype(BF16)

    z_ref[...] = proj(P_Z, SSM_INNER)
    xbc_ref[...] = proj(P_XBC, XBC_WIDTH)
    dt_ref[...] = proj(P_DT, 2 * LANE)

    ca = ca_ref[...]
    cb = cb_ref[...]
    scale = np.float32((QK_NOPE + QK_ROPE) ** -0.5)

    def rope(t):
        return t * ca + pltpu.roll(t, QK_ROPE, 1) * cb

    cqn = _rms_norm(proj(P_Q, Q_LORA), gq_ref[...]).astype(BF16)
    yq = jnp.dot(cqn, wq_ref[...], preferred_element_type=F32)
    for hh in range(MLA_HEADS):
        c0 = hh * HEAD_PAD
        q_ref[:, c0:c0 + LANE] = (yq[:, c0:c0 + LANE] * scale).astype(BF16)
        q_ref[:, c0 + LANE:c0 + HEAD_PAD] = (rope(yq[:, c0 + LANE:c0 + HEAD_PAD]) * scale).astype(BF16)

    ckvn = _rms_norm(proj(P_KV, KV_LORA), gkv_ref[...]).astype(BF16)
    krf = rope(proj(P_KR, LANE)).astype(BF16)
    ykv = jnp.dot(ckvn, wkv_ref[...], preferred_element_type=F32)
    for hh in range(MLA_HEADS):
        c0 = hh * HEAD_PAD
        k_ref[:, c0:c0 + LANE] = ykv[:, hh * LANE:(hh + 1) * LANE].astype(BF16)
        k_ref[:, c0 + LANE:c0 + HEAD_PAD] = krf
    v_ref[...] = ykv[:, MLA_HEADS * QK_NOPE:].astype(BF16)


def _proj_in(x_all, modv, ca, cb, wp, dims):
    rows = x_all.shape[0]
    tm = ROW_TILE
    nt = rows // tm
    lat_tiles = dims["seq"] // tm
    ctx_tiles = dims["ctx"] // tm
    kv_tiles = lat_tiles + ctx_tiles
    nb = dims["batch"]

    def mod_map(i):
        return (jnp.minimum(i // lat_tiles, nb), 0, 0)

    def kv_map(i):
        lat_b = i // lat_tiles
        lat_pos = lat_b * kv_tiles + ctx_tiles + i % lat_tiles
        j = i - nb * lat_tiles
        ctx_pos = (j // ctx_tiles) * kv_tiles + j % ctx_tiles
        return (jnp.where(i < nb * lat_tiles, lat_pos, ctx_pos), 0)

    row = lambda w: pl.BlockSpec((tm, w), lambda i: (i, 0))
    kv_rows = nb * (dims["seq"] + dims["ctx"])
    return pl.pallas_call(
        _proj_in_kernel,
        grid=(nt,),
        in_specs=[
            row(D_MODEL),
            pl.BlockSpec((1, 6, D_MODEL), mod_map),
            row(LANE), row(LANE),
            _resident((D_MODEL, P_WIDTH)),
            _resident((1, GM_WIDTH)), _resident((1, GM_WIDTH)),
            _resident((GM_HEADS, GM_CHUNK, GM_CHUNK)), _resident((GM_CHUNK, GM_WIDTH)),
            _resident((1, Q_LORA)), _resident((Q_LORA, MLA_HEADS * HEAD_PAD)),
            _resident((1, KV_LORA)), _resident((KV_LORA, MLA_HEADS * (QK_NOPE + V_DIM))),
        ],
        out_specs=[
            row(GM_WIDTH), row(SSM_INNER), row(XBC_WIDTH), row(2 * LANE),
            row(MLA_HEADS * HEAD_PAD),
            pl.BlockSpec((tm, MLA_HEADS * HEAD_PAD), kv_map),
            pl.BlockSpec((tm, MLA_HEADS * V_DIM), kv_map),
        ],
        out_shape=[
            jax.ShapeDtypeStruct((rows, GM_WIDTH), BF16),
            jax.ShapeDtypeStruct((rows, SSM_INNER), F32),
            jax.ShapeDtypeStruct((rows, XBC_WIDTH), F32),
            jax.ShapeDtypeStruct((rows, 2 * LANE), F32),
            jax.ShapeDtypeStruct((rows, MLA_HEADS * HEAD_PAD), BF16),
            jax.ShapeDtypeStruct((kv_rows, MLA_HEADS * HEAD_PAD), BF16),
            jax.ShapeDtypeStruct((kv_rows, MLA_HEADS * V_DIM), BF16),
        ],
        compiler_params=_cparams(("arbitrary",)),
        name="proj_in",
    )(x_all, modv, ca, cb, wp["w_in"], wp["g_v"], wp["b_v"], wp["w_sp"], wp["b_sp"],
      wp["g_q"], wp["w_q"], wp["g_kv"], wp["w_kv"])


def _ssd_kernel(backward, n_ctx, n_lat, *refs):
    if backward:
        (xp_ref, xc_ref, xn_ref, dt_ref, cw_ref, cbias_ref, dtb_ref, alog_ref, e_ref,
         yf_ref, z_ref, skip_ref, gssm_ref, out_ref, h_scr) = refs
    else:
        (xp_ref, xc_ref, xn_ref, dt_ref, cw_ref, cbias_ref, dtb_ref, alog_ref, e_ref,
         out_ref, h_scr) = refs
    q = SSM_CHUNK
    s = pl.program_id(1)
    is_ctx = s < n_ctx
    if backward:
        cc = jnp.where(is_ctx, n_ctx - 1 - s, n_lat - 1 - (s - n_ctx))
    else:
        cc = jnp.where(is_ctx, s, s - n_ctx)
    ncc = jnp.where(is_ctx, n_ctx, n_lat)
    at_start = cc == 0
    at_end = cc == ncc - 1

    @pl.when(s == 0)
    def _():
        h_scr[...] = jnp.zeros_like(h_scr)

    prev = jnp.where(at_start, 0.0, xp_ref[...])
    nxt = jnp.where(at_end, 0.0, xn_ref[...])
    ext = jnp.concatenate([prev, xc_ref[...], nxt], axis=0)
    acc = jnp.broadcast_to(cbias_ref[...], (q, XBC_WIDTH))
    base = 8 - SSM_CONV // 2
    for j in range(SSM_CONV):
        acc = acc + cw_ref[j:j + 1, :] * ext[base + j:base + j + q, :]
    u = _silu(acc)
    xc = u[:, :SSM_INNER]

    lane = lax.broadcasted_iota(jnp.int32, (q, LANE), 1)
    dtr = dt_ref[...] + dtb_ref[...]
    dtv = jnp.maximum(dtr, 0.0) + jnp.log1p(jnp.exp(-jnp.abs(dtr)))
    dtv = jnp.where(lane < SSM_HEADS, dtv, 0.0)
    da = dtv * (-jnp.exp(alog_ref[...]))

    ri = lax.broadcasted_iota(jnp.int32, (q, q), 0)
    ci = lax.broadcasted_iota(jnp.int32, (q, q), 1)
    mask = (ci >= ri) if backward else (ci <= ri)
    tri = jnp.where(mask, 1.0, 0.0).astype(F32)
    acs = jnp.dot(tri, da, precision=HIGHEST, preferred_element_type=F32)
    acs_t = acs.T
    last = 0 if backward else q - 1
    acs_last = acs[last:last + 1, :]
    small = jnp.concatenate([dtv, jnp.exp(acs_last - acs), jnp.exp(acs)], axis=0)
    wide = jnp.dot(small, e_ref[...], precision=HIGHEST, preferred_element_type=F32)
    dt_w = wide[:q]
    decay_w = wide[q:2 * q]
    eacs_w = wide[2 * q:]
    elast_w = eacs_w[last:last + 1, :]

    xdt = xc * dt_w
    xw_b = (xdt * decay_w).astype(BF16)
    h_old = h_scr[...]
    h_b = h_old.astype(BF16)

    lane_w = lax.broadcasted_iota(jnp.int32, (q, SSM_INNER), 1) // SSM_HEAD_DIM
    w_parts = []
    x_parts = []
    inter = []
    upd = []
    per_g = SSM_HEADS // SSM_GROUPS
    gw = per_g * SSM_HEAD_DIM
    for g in range(SSM_GROUPS):
        b_g = u[:, SSM_INNER + g * SSM_STATE:SSM_INNER + (g + 1) * SSM_STATE].astype(BF16)
        o_c = SSM_INNER + SSM_GROUPS * SSM_STATE
        c_g = u[:, o_c + g * SSM_STATE:o_c + (g + 1) * SSM_STATE].astype(BF16)
        cbm = lax.dot_general(c_g, b_g, (((1,), (1,)), ((), ())), preferred_element_type=F32)
        for hh in range(per_g):
            hd = g * per_g + hh
            seg = acs[:, hd:hd + 1] - acs_t[hd:hd + 1, :]
            lmat = jnp.exp(jnp.where(mask, seg, -jnp.inf))
            w_parts.append((cbm * lmat).astype(BF16))
            x_parts.append(jnp.where(lane_w == hd, xdt, 0.0).astype(BF16))
        inter.append(jnp.dot(c_g, h_b[:, g * gw:(g + 1) * gw], preferred_element_type=F32))
        upd.append(lax.dot_general(b_g, xw_b[:, g * gw:(g + 1) * gw], (((0,), (0,)), ((), ())),
                                   preferred_element_type=F32))
    y = jnp.dot(jnp.concatenate(w_parts, axis=1), jnp.concatenate(x_parts, axis=0),
                preferred_element_type=F32)
    y = y + jnp.concatenate(inter, axis=1) * eacs_w
    h_scr[...] = h_old * elast_w + jnp.concatenate(upd, axis=1)

    if backward:
        tot = yf_ref[...] + y + skip_ref[...] * xc
        yz = tot * _silu(z_ref[...])
        half = SSM_INNER // SSM_GROUPS
        outs = []
        for g in range(SSM_GROUPS):
            grp = yz[:, g * half:(g + 1) * half]
            outs.append(grp * lax.rsqrt(jnp.mean(grp * grp, axis=-1, keepdims=True) + RMS_EPS))
        out_ref[...] = (jnp.concatenate(outs, axis=1) * gssm_ref[...]).astype(BF16)
    else:
        out_ref[...] = y


def _ssd_sweep(backward, xbc, dtc, wp, dims, yf=None, z=None):
    nb = dims["batch"]
    n_lat = dims["seq"] // SSM_CHUNK
    n_ctx = dims["ctx"] // SSM_CHUNK
    ctx_base = nb * n_lat
    rows = xbc.shape[0]
    sub = SSM_CHUNK // 8

    def blk(b, s):
        is_ctx = s < n_ctx
        if backward:
            cc = jnp.where(is_ctx, n_ctx - 1 - s, n_lat - 1 - (s - n_ctx))
        else:
            cc = jnp.where(is_ctx, s, s - n_ctx)
        return jnp.where(is_ctx, ctx_base + b * n_ctx + cc, b * n_lat + cc)

    cur = lambda w: pl.BlockSpec((SSM_CHUNK, w), lambda b, s: (blk(b, s), 0))
    d = 1 if backward else 0
    in_specs = [
        pl.BlockSpec((8, XBC_WIDTH), lambda b, s: (jnp.maximum(blk(b, s) * sub - 1, 0), 0)),
        cur(XBC_WIDTH),
        pl.BlockSpec((8, XBC_WIDTH), lambda b, s: (jnp.minimum((blk(b, s) + 1) * sub, rows // 8 - 1), 0)),
        pl.BlockSpec((SSM_CHUNK, LANE), lambda b, s: (blk(b, s), d)),
        _resident((8, XBC_WIDTH)), _resident((1, XBC_WIDTH)),
        _resident((1, LANE)), _resident((1, LANE)), _resident((LANE, SSM_INNER)),
    ]
    args = [xbc, xbc, xbc, dtc, wp["conv_w"], wp["conv_b"], wp["dt_bias"][d], wp["a_log"][d], wp["expand"]]
    if backward:
        in_specs += [cur(SSM_INNER), cur(SSM_INNER), _resident((1, SSM_INNER)), _resident((1, SSM_INNER))]
        args += [yf, z, wp["d_skip"], wp["g_ssm"]]
    return pl.pallas_call(
        functools.partial(_ssd_kernel, backward, n_ctx, n_lat),
        grid=(nb, n_ctx + n_lat),
        in_specs=in_specs,
        out_specs=cur(SSM_INNER),
        out_shape=jax.ShapeDtypeStruct((rows, SSM_INNER), BF16 if backward else F32),
        scratch_shapes=[pltpu.VMEM((SSM_STATE, SSM_INNER), F32)],
        compiler_params=_cparams(("arbitrary", "arbitrary")),
        name="ssd_bwd" if backward else "ssd_fwd",
    )(*args)


def _attn_kernel(nk, tk, q_ref, k_ref, v_ref, o_ref, m_scr, l_scr, acc_scr):
    q = q_ref[...]
    m_scr[...] = jnp.full_like(m_scr, -jnp.inf)
    l_scr[...] = jnp.zeros_like(l_scr)
    acc_scr[...] = jnp.zeros_like(acc_scr)

    def body(i, carry):
        off = pl.multiple_of(i * tk, tk)
        k = k_ref[pl.ds(off, tk), :]
        v = v_ref[pl.ds(off, tk), :]
        s = lax.dot_general(q, k, (((1,), (1,)), ((), ())), preferred_element_type=F32)
        m_prev = m_scr[...]
        m_new = jnp.maximum(m_prev, jnp.max(s, axis=1, keepdims=True))
        alpha = jnp.exp(m_prev - m_new)
        p = jnp.exp(s - pltpu.repeat(m_new, tk // LANE, 1))
        l_scr[...] = alpha * l_scr[...] + jnp.sum(p, axis=1, keepdims=True)
        acc_scr[...] = alpha * acc_scr[...] + jnp.dot(p.astype(BF16), v, preferred_element_type=F32)
        m_scr[...] = m_new
        return carry

    lax.fori_loop(0, nk, body, 0)
    o_ref[...] = (acc_scr[...] / l_scr[...]).astype(BF16)


def _attn_into_kernel(nk, tk, q_ref, k_ref, v_ref, base_ref, o_ref, m_scr, l_scr, acc_scr):
    del base_ref
    _attn_kernel(nk, tk, q_ref, k_ref, v_ref, o_ref, m_scr, l_scr, acc_scr)


def _attention(q, k, v, nb, lq, lk, q_row0, kv_stride, tq, tk, into=None):
    nq = lq // tq
    q0 = q_row0 // tq
    out_rows = q.shape[0]
    assert kv_stride % lk == 0 and lk % tk == 0 and q_row0 % tq == 0
    kvb = kv_stride // lk
    in_specs = [
        pl.BlockSpec((tq, HEAD_PAD), lambda b, h, i: (q0 + b * nq + i, h)),
        pl.BlockSpec((lk, HEAD_PAD), lambda b, h, i: (b * kvb, h)),
        pl.BlockSpec((lk, V_DIM), lambda b, h, i: (b * kvb, h)),
    ]
    args = [q, k, v]
    body = _attn_kernel
    aliases = {}
    if into is not None:
        in_specs.append(pl.BlockSpec(memory_space=pl.ANY))
        args.append(into)
        body = _attn_into_kernel
        aliases = {3: 0}
    return pl.pallas_call(
        functools.partial(body, lk // tk, tk),
        grid=(nb, MLA_HEADS, nq),
        in_specs=in_specs,
        out_specs=pl.BlockSpec((tq, V_DIM), lambda b, h, i: (q0 + b * nq + i, h)),
        out_shape=jax.ShapeDtypeStruct((out_rows, MLA_HEADS * V_DIM), BF16),
        scratch_shapes=[pltpu.VMEM((tq, LANE), F32), pltpu.VMEM((tq, LANE), F32), pltpu.VMEM((tq, V_DIM), F32)],
        input_output_aliases=aliases,
        compiler_params=_cparams(("arbitrary", "arbitrary", "arbitrary")),
        name="mla_attention",
    )(*args)


def _out_proj_kernel(gm_ref, ssm_ref, att_ref, x_ref, mod_ref, w_ref, g_ref, b_ref, wr_ref,
                     x1_ref, tok_ref, lg_ref):
    mod = mod_ref[0]
    mixed = jnp.concatenate([gm_ref[...], ssm_ref[...], att_ref[...]], axis=1)
    mix = jnp.dot(mixed, w_ref[...], preferred_element_type=F32)
    x1 = _layer_norm(DN_ALPHA * x_ref[...] + mod[2:3, :] * mix, g_ref[...], b_ref[...])
    x1_ref[...] = x1
    tok = x1 * (1.0 + mod[4:5, :]) + mod[3:4, :]
    tok_ref[...] = tok.astype(BF16)
    lg_ref[...] = lax.dot_general(wr_ref[...], tok, (((1,), (1,)), ((), ())),
                                  precision=HIGHEST, preferred_element_type=F32)


def _out_proj(gm, ssm, att, x_all, modv, wp, dims, rows):
    tm = ROW_TILE
    lat_tiles = dims["seq"] // tm
    nb = dims["batch"]
    row = lambda w: pl.BlockSpec((tm, w), lambda i: (i, 0))
    return pl.pallas_call(
        _out_proj_kernel,
        grid=(rows // tm,),
        in_specs=[
            row(GM_WIDTH), row(SSM_INNER), row(MLA_HEADS * V_DIM), row(D_MODEL),
            pl.BlockSpec((1, 6, D_MODEL), lambda i: (jnp.minimum(i // lat_tiles, nb), 0, 0)),
            _resident((D_MODEL, D_MODEL)), _resident((1, D_MODEL)), _resident((1, D_MODEL)),
            _resident((N_EXPERTS, D_MODEL)),
        ],
        out_specs=[row(D_MODEL), row(D_MODEL), pl.BlockSpec((N_EXPERTS, tm), lambda i: (0, i))],
        out_shape=[
            jax.ShapeDtypeStruct((rows, D_MODEL), F32),
            jax.ShapeDtypeStruct((rows, D_MODEL), BF16),
            jax.ShapeDtypeStruct((N_EXPERTS, rows), F32),
        ],
        compiler_params=_cparams(("arbitrary",)),
        name="out_proj",
    )(gm, ssm, att, x_all, modv, wp["w_out"], wp["ln1_g"], wp["ln1_b"], wp["w_router_t"])


def _gate_kernel(lg_ref, b_ref, o_ref):
    per = N_EXPERTS // N_EXPERT_GROUPS
    tn = lg_ref.shape[1]
    sc = jax.nn.sigmoid(lg_ref[...])
    bi = sc + b_ref[...]
    neg = np.float32(-np.inf)
    xs = [bi[j * N_EXPERT_GROUPS:(j + 1) * N_EXPERT_GROUPS, :] for j in range(per)]
    ss = [sc[j * N_EXPERT_GROUPS:(j + 1) * N_EXPERT_GROUPS, :] for j in range(per)]
    m1 = functools.reduce(jnp.maximum, xs)
    first = functools.reduce(jnp.minimum, [jnp.where(xs[j] == m1, j, per) for j in range(per)])
    m2 = functools.reduce(jnp.maximum, [jnp.where(first == j, neg, xs[j]) for j in range(per)])
    gscore = m1 + m2
    gi = lax.broadcasted_iota(jnp.int32, (N_EXPERT_GROUPS, tn), 0)
    cur = gscore
    gsel = jnp.zeros((N_EXPERT_GROUPS, tn), jnp.int32)
    for _ in range(TOPK_GROUPS):
        mx = jnp.max(cur, axis=0, keepdims=True)
        fi = jnp.min(jnp.where(cur == mx, gi, N_EXPERT_GROUPS), axis=0, keepdims=True)
        pick = gi == fi
        gsel = jnp.where(pick, 1, gsel)
        cur = jnp.where(pick, neg, cur)
    cur = [jnp.where(gsel > 0, xs[j], neg) for j in range(per)]
    eid = [gi * per + j for j in range(per)]
    sel = [jnp.zeros((N_EXPERT_GROUPS, tn), jnp.int32) for _ in range(per)]
    for _ in range(TOP_K):
        mx = jnp.max(functools.reduce(jnp.maximum, cur), axis=0, keepdims=True)
        cand = functools.reduce(jnp.minimum, [jnp.where(cur[j] == mx, eid[j], N_EXPERTS) for j in range(per)])
        fi = jnp.min(cand, axis=0, keepdims=True)
        for j in range(per):
            pick = eid[j] == fi
            sel[j] = jnp.where(pick, 1, sel[j])
            cur[j] = jnp.where(pick, neg, cur[j])
    w = [jnp.where(sel[j] > 0, ss[j], 0.0) for j in range(per)]
    tot = jnp.sum(functools.reduce(jnp.add, w), axis=0, keepdims=True)
    for j in range(per):
        o_ref[j * N_EXPERT_GROUPS:(j + 1) * N_EXPERT_GROUPS, :] = w[j] / tot * ROUTED_SCALE
        o_ref[N_EXPERTS + j * N_EXPERT_GROUPS:N_EXPERTS + (j + 1) * N_EXPERT_GROUPS, :] = sel[j].astype(F32)


def _gate(logits_t, b_router_col):
    t = logits_t.shape[1]
    tn = 512
    return pl.pallas_call(
        _gate_kernel,
        grid=(t // tn,),
        in_specs=[pl.BlockSpec((N_EXPERTS, tn), lambda i: (0, i)), _resident((N_EXPERTS, 1))],
        out_specs=pl.BlockSpec((2 * N_EXPERTS, tn), lambda i: (0, i)),
        out_shape=jax.ShapeDtypeStruct((2 * N_EXPERTS, t), F32),
        compiler_params=_cparams(("arbitrary",)),
        name="gate_topk",
    )(logits_t, b_router_col)


def _experts_kernel(be_ref, nu_ref, x_ref, sw_ref, wg_ref, wu_ref, wd_ref, y_ref, wg_s, wu_s, wd_s):
    j = pl.program_id(0)
    prev = be_ref[jnp.maximum(j - 1, 0)]
    fresh = jnp.logical_or(j == 0, be_ref[j] != prev)

    @pl.when(fresh)
    def _():
        wg_s[...] = wg_ref[0].astype(BF16)
        wu_s[...] = wu_ref[0].astype(BF16)
        wd_s[...] = wd_ref[0].astype(BF16)

    @pl.when(j < nu_ref[0])
    def _():
        x = x_ref[...]
        g = jnp.dot(x, wg_s[...], preferred_element_type=F32)
        u = jnp.dot(x, wu_s[...], preferred_element_type=F32)
        hb = (_silu(g) * u).astype(BF16)
        y_ref[...] = jnp.dot(hb, wd_s[...], preferred_element_type=F32) * sw_ref[...]

    @pl.when(j >= nu_ref[0])
    def _():
        y_ref[...] = jnp.zeros_like(y_ref)


def _experts(layer, block_e, n_used, x_sorted, slot_w, w_gate, w_up, w_down):
    cap = x_sorted.shape[0]
    blk = EXPERT_BLOCK
    return pl.pallas_call(
        _experts_kernel,
        grid_spec=pltpu.PrefetchScalarGridSpec(
            num_scalar_prefetch=2,
            grid=(cap // blk,),
            in_specs=[
                pl.BlockSpec((blk, D_MODEL), lambda j, be, nu: (j, 0)),
                pl.BlockSpec((blk, 1), lambda j, be, nu: (j, 0)),
                pl.BlockSpec((None, 1, D_MODEL, EXPERT_FF), lambda j, be, nu: (layer, be[j], 0, 0)),
                pl.BlockSpec((None, 1, D_MODEL, EXPERT_FF), lambda j, be, nu: (layer, be[j], 0, 0)),
                pl.BlockSpec((None, 1, EXPERT_FF, D_MODEL), lambda j, be, nu: (layer, be[j], 0, 0)),
            ],
            out_specs=pl.BlockSpec((blk, D_MODEL), lambda j, be, nu: (j, 0)),
            scratch_shapes=[pltpu.VMEM((D_MODEL, EXPERT_FF), BF16), pltpu.VMEM((D_MODEL, EXPERT_FF), BF16),
                            pltpu.VMEM((EXPERT_FF, D_MODEL), BF16)],
        ),
        out_shape=jax.ShapeDtypeStruct((cap, D_MODEL), F32),
        compiler_params=_cparams(("arbitrary",)),
        name="routed_experts",
    )(block_e, n_used, x_sorted, slot_w, w_gate, w_up, w_down)


def _final_kernel(tok_ref, routed_ref, x1_ref, mod_ref, wgu_ref, wd_ref, g_ref, b_ref, o_ref):
    mod = mod_ref[0]
    gu = jnp.dot(tok_ref[...], wgu_ref[...], preferred_element_type=F32)
    hb = (_silu(gu[:, :SHARED_FF]) * gu[:, SHARED_FF:]).astype(BF16)
    f = routed_ref[...] + jnp.dot(hb, wd_ref[...], preferred_element_type=F32)
    o_ref[...] = _layer_norm(DN_ALPHA * x1_ref[...] + mod[5:6, :] * f, g_ref[...], b_ref[...])


def _final(tok, routed, x1, modv, wp, dims, rows):
    tm = ROW_TILE
    lat_tiles = dims["seq"] // tm
    nb = dims["batch"]
    row = lambda w: pl.BlockSpec((tm, w), lambda i: (i, 0))
    return pl.pallas_call(
        _final_kernel,
        grid=(rows // tm,),
        in_specs=[
            row(D_MODEL), row(D_MODEL), row(D_MODEL),
            pl.BlockSpec((1, 6, D_MODEL), lambda i: (jnp.minimum(i // lat_tiles, nb), 0, 0)),
            _resident((D_MODEL, 2 * SHARED_FF)), _resident((SHARED_FF, D_MODEL)),
            _resident((1, D_MODEL)), _resident((1, D_MODEL)),
        ],
        out_specs=row(D_MODEL),
        out_shape=jax.ShapeDtypeStruct((rows, D_MODEL), F32),
        compiler_params=_cparams(("arbitrary",)),
        name="shared_ffn_ln2",
    )(tok, routed, x1, modv, wp["w_sh_gu"], wp["w_sh_down"], wp["ln2_g"], wp["ln2_b"])


def _rot_half_cols():
    src = np.zeros((QK_ROPE,), np.int32)
    sign = np.zeros((QK_ROPE,), np.float32)
    quarter = QK_ROPE // 4
    for dcol in range(QK_ROPE):
        part, i = divmod(dcol, 2 * quarter)
        half, kk = divmod(i, quarter)
        src[dcol] = part * 2 * quarter + (quarter + kk if half == 0 else kk)
        sign[dcol] = -1.0 if half == 0 else 1.0
    return src, sign


def _expert_major(a, axis):
    per = N_EXPERTS // N_EXPERT_GROUPS
    shp = a.shape
    a = a.reshape(shp[:axis] + (N_EXPERT_GROUPS, per) + shp[axis + 1:])
    a = jnp.swapaxes(a, axis, axis + 1)
    return a.reshape(shp)


def _prep_layer(l, p):
    src, sign = _rot_half_cols()
    w_in = p["w_in"][l]
    col = lambda lo, n: w_in[:, lo:lo + n]
    zpad = jnp.zeros((D_MODEL, LANE - SSM_HEADS), F32)
    w_kr = col(R_KR, QK_ROPE)
    w_in_p = jnp.concatenate([
        col(R_GM, 2 * GM_WIDTH), col(R_Q, Q_LORA), col(R_Z, SSM_INNER), col(R_XBC, XBC_WIDTH),
        col(R_DT, SSM_HEADS), zpad, col(R_DT + SSM_HEADS, SSM_HEADS), zpad,
        col(R_KV, KV_LORA), w_kr, w_kr[:, src] * sign,
    ], axis=1).astype(BF16)
    wq = p["w_q_b"][l].reshape(Q_LORA, MLA_HEADS, QK_NOPE + QK_ROPE)
    wq_rope = wq[:, :, QK_NOPE:]
    w_q = jnp.concatenate([wq[:, :, :QK_NOPE], wq_rope, wq_rope[:, :, src] * sign], axis=2)
    w_q = w_q.reshape(Q_LORA, MLA_HEADS * HEAD_PAD).astype(BF16)
    wkv = p["w_kv_b"][l].reshape(KV_LORA, MLA_HEADS, QK_NOPE + V_DIM)
    w_kv = jnp.concatenate([wkv[:, :, :QK_NOPE].reshape(KV_LORA, -1), wkv[:, :, QK_NOPE:].reshape(KV_LORA, -1)],
                           axis=1).astype(BF16)
    lane_pad = lambda a: jnp.pad(a, ((0, 0), (0, LANE - a.shape[1])))
    expand = jnp.repeat(jnp.eye(LANE, SSM_HEADS, dtype=F32), SSM_HEAD_DIM, axis=1)
    r2 = lambda a: a.reshape(1, -1)
    return dict(
        w_in=w_in_p, g_v=r2(p["g_v"][l]), b_v=r2(p["b_v"][l]), w_sp=p["w_sp"][l].astype(BF16),
        b_sp=jnp.repeat(p["b_sp"][l].T, GM_CHUNK, axis=1),
        g_q=r2(p["g_q"][l]), w_q=w_q, g_kv=r2(p["g_kv"][l]), w_kv=w_kv,
        conv_w=jnp.pad(p["conv_w"][l], ((0, 8 - SSM_CONV), (0, 0))), conv_b=r2(p["conv_b"][l]),
        dt_bias=lane_pad(p["dt_bias"][l])[:, None, :], a_log=lane_pad(p["a_log"][l])[:, None, :],
        expand=expand, d_skip=r2(jnp.repeat(p["d_skip"][l], SSM_HEAD_DIM)), g_ssm=r2(p["g_ssm"][l]),
        w_out=p["w_out"][l].astype(BF16), ln1_g=r2(p["ln1_g"][l]), ln1_b=r2(p["ln1_b"][l]),
        ln2_g=r2(p["ln2_g"][l]), ln2_b=r2(p["ln2_b"][l]),
        w_router_t=_expert_major(p["w_router"][l].T, 0),
        b_router=_expert_major(p["b_router"][l], 0).reshape(N_EXPERTS, 1),
        w_sh_gu=jnp.concatenate([p["w_sh_gate"][l], p["w_sh_up"][l]], axis=1).astype(BF16),
        w_sh_down=p["w_sh_down"][l].astype(BF16),
    )


def _rope_tables(nb, seq, nctx):
    t = jnp.arange(seq)
    half = QK_ROPE // 2
    inv = 1.0 / (ROPE_BASE ** (jnp.arange(0, half, 2, dtype=F32) / half))
    ang_r = (t // GRID_W)[:, None] * inv
    ang_c = (t % GRID_W)[:, None] * inv
    ang = jnp.concatenate([ang_r, ang_r, ang_c, ang_c], axis=1)
    zero = jnp.zeros((seq, LANE - QK_ROPE), F32)
    ca_lat = jnp.concatenate([jnp.cos(ang), zero], axis=1)
    cb_lat = jnp.concatenate([jnp.sin(ang), zero], axis=1)
    ca_ctx = jnp.concatenate([jnp.ones((nb * nctx, QK_ROPE), F32), jnp.zeros((nb * nctx, LANE - QK_ROPE), F32)], axis=1)
    ca = jnp.concatenate([jnp.tile(ca_lat, (nb, 1)), ca_ctx], axis=0)
    cb = jnp.concatenate([jnp.tile(cb_lat, (nb, 1)), jnp.zeros((nb * nctx, LANE), F32)], axis=0)
    return ca, cb


def _dispatch(gate_out, t):
    per = N_EXPERTS // N_EXPERT_GROUPS
    to_tok = lambda a: a.reshape(per, N_EXPERT_GROUPS, t).transpose(2, 1, 0).reshape(t, N_EXPERTS)
    gates = to_tok(gate_out[:N_EXPERTS])
    selm = to_tok(gate_out[N_EXPERTS:]) > 0.5
    rank = jnp.cumsum(selm.astype(jnp.int32), axis=1) - 1
    koh = (rank[:, :, None] == jnp.arange(TOP_K)) & selm[:, :, None]
    eidx = jnp.sum(jnp.where(koh, jnp.arange(N_EXPERTS, dtype=jnp.int32)[None, :, None], 0), axis=1)
    w8 = jnp.sum(jnp.where(koh, gates[:, :, None], 0.0), axis=1)
    n_assign = t * TOP_K
    blk = EXPERT_BLOCK
    flat_e = eidx.reshape(-1)
    flat_tok = jnp.repeat(jnp.arange(t, dtype=jnp.int32), TOP_K)
    order = jnp.argsort(flat_e, stable=True)
    e_sorted = flat_e[order]
    counts = jnp.sum(selm.astype(jnp.int32), axis=0)
    padded = (counts + blk - 1) // blk * blk
    start = jnp.cumsum(counts) - counts
    pend = jnp.cumsum(padded)
    pstart = pend - padded
    dest = (pstart[e_sorted] + jnp.arange(n_assign, dtype=jnp.int32) - start[e_sorted]).astype(jnp.int32)
    n_blocks = (n_assign + N_EXPERTS * (blk - 1) + blk - 1) // blk
    cap = n_blocks * blk
    slot_tok = jnp.zeros((cap,), jnp.int32).at[dest].set(flat_tok[order])
    slot_w = jnp.zeros((cap,), F32).at[dest].set(w8.reshape(-1)[order])
    block_e = jnp.minimum(jnp.searchsorted(pend, jnp.arange(n_blocks) * blk, side="right"), N_EXPERTS - 1)
    n_used = (pend[-1] // blk).reshape(1)
    slot_of = jnp.zeros((n_assign,), jnp.int32).at[order].set(dest).reshape(t, TOP_K)
    return slot_tok, slot_w.reshape(cap, 1), block_e.astype(jnp.int32), n_used.astype(jnp.int32), slot_of


def kernel(x, c, ctx, c_ctx, w_mod, b_mod, w_in, g_q, w_q_b, g_kv, w_kv_b, conv_w, conv_b, a_log, dt_bias, d_skip, g_ssm, g_v, b_v, w_sp, b_sp, w_out, ln1_g, ln1_b, ln2_g, ln2_b, w_router, b_router, w_e_gate, w_e_up, w_e_down, w_sh_gate, w_sh_up, w_sh_down):
    p = dict(w_in=w_in, g_q=g_q, w_q_b=w_q_b, g_kv=g_kv, w_kv_b=w_kv_b, conv_w=conv_w, conv_b=conv_b,
             a_log=a_log, dt_bias=dt_bias, d_skip=d_skip, g_ssm=g_ssm, g_v=g_v, b_v=b_v, w_sp=w_sp, b_sp=b_sp,
             w_out=w_out, ln1_g=ln1_g, ln1_b=ln1_b, ln2_g=ln2_g, ln2_b=ln2_b, w_router=w_router,
             b_router=b_router, w_sh_gate=w_sh_gate, w_sh_up=w_sh_up, w_sh_down=w_sh_down)
    nb, seq, d = x.shape
    nctx = ctx.shape[1]
    depth = w_mod.shape[0]
    assert d == D_MODEL and depth == DEPTH and nb + 1 <= 8
    assert seq % ROW_TILE == 0 and nctx % ROW_TILE == 0 and seq % GRID_W == 0
    dims = dict(batch=nb, seq=seq, ctx=nctx)
    lat_rows = nb * seq
    rows_all = lat_rows + nb * nctx
    lk = seq + nctx

    x_all = jnp.concatenate([x.reshape(lat_rows, d), ctx.reshape(nb * nctx, d)], axis=0)
    cvec = jnp.zeros((8, d), F32).at[:nb].set(c).at[nb].set(c_ctx)
    mods = _modulation(cvec, w_mod, b_mod).reshape(depth, 8, 6, d)[:, :nb + 1]
    ca, cb = _rope_tables(nb, seq, nctx)
    tq = min(512, seq)
    tk = 768 if lk % 768 == 0 else ROW_TILE

    for l in range(depth):
        last = l == depth - 1
        wp = _prep_layer(l, p)
        modv = mods[l]
        gm, z, xbc, dtc, q, k, v = _proj_in(x_all, modv, ca, cb, wp, dims)
        yf = _ssd_sweep(False, xbc, dtc, wp, dims)
        ssm = _ssd_sweep(True, xbc, dtc, wp, dims, yf=yf, z=z)
        att = _attention(q, k, v, nb, seq, lk, 0, lk, tq, tk)
        rows = lat_rows if last else rows_all
        if not last:
            att = _attention(q, k, v, nb, nctx, nctx, lat_rows, lk, ROW_TILE, ROW_TILE, into=att)
        x1, tok, logits_t = _out_proj(gm, ssm, att, x_all, modv, wp, dims, rows)
        gate_out = _gate(logits_t, wp["b_router"])
        slot_tok, slot_w, block_e, n_used, slot_of = _dispatch(gate_out, rows)
        x_sorted = tok[slot_tok]
        y_sorted = _experts(l, block_e, n_used, x_sorted, slot_w, w_e_gate, w_e_up, w_e_down)
        routed = jnp.sum(y_sorted[slot_of], axis=1)
        x_all = _final(tok, routed, x1, modv, wp, dims, rows)
    return x_all.reshape(nb, seq, d)
```

```python
import functools
import math

import jax
import jax.numpy as jnp
import numpy as np
from jax import lax
from jax.experimental import pallas as pl
from jax.experimental.pallas import tpu as pltpu

F32 = jnp.float32
BF16 = jnp.bfloat16
HIGHEST = lax.Precision.HIGHEST

D_MODEL = 2048
GRID_W = 64
GM_HEADS = 4
GM_WIDTH = 512
GM_CHUNK = 128
SSM_HEADS = 8
SSM_HEAD_DIM = 64
SSM_INNER = 512
SSM_GROUPS = 2
SSM_STATE = 128
SSM_CONV = 5
SSM_CHUNK = 128
XBC_WIDTH = 1024
MLA_HEADS = 8
QK_NOPE = 128
QK_ROPE = 64
V_DIM = 128
Q_LORA = 768
KV_LORA = 256
ROPE_BASE = 10000.0
N_EXPERTS = 64
TOP_K = 8
N_EXPERT_GROUPS = 8
TOPK_GROUPS = 4
EXPERT_FF = 512
SHARED_FF = 512
ROUTED_SCALE = 2.5
LN_EPS = 1e-5
RMS_EPS = 1e-6
DEPTH = 2
DN_ALPHA = (2 * DEPTH) ** 0.25

R_GM = 0
R_Q = R_GM + 2 * GM_WIDTH
R_Z = R_Q + Q_LORA
R_XBC = R_Z + SSM_INNER
R_DT = R_XBC + XBC_WIDTH
R_KV = R_DT + 2 * SSM_HEADS
R_KR = R_KV + KV_LORA

LANE = 128
P_GM = 0
P_Q = P_GM + 2 * GM_WIDTH
P_Z = P_Q + Q_LORA
P_XBC = P_Z + SSM_INNER
P_DT = P_XBC + XBC_WIDTH
P_KV = P_DT + 2 * LANE
P_KR = P_KV + KV_LORA
P_WIDTH = P_KR + LANE

HEAD_PAD = 256
ROW_TILE = 256
EXPERT_BLOCK = 256
VMEM_LIMIT = 56 * 1024 * 1024


def _cparams(sem):
    return pltpu.CompilerParams(dimension_semantics=sem, vmem_limit_bytes=VMEM_LIMIT)


def _resident(shape):
    n = len(shape)
    return pl.BlockSpec(shape, lambda *_: (0,) * n, pipeline_mode=pl.Buffered(1))


def _silu(x):
    return x * jax.nn.sigmoid(x)


def _layer_norm(y, g, b):
    mu = jnp.mean(y, axis=-1, keepdims=True)
    yc = y - mu
    var = jnp.mean(yc * yc, axis=-1, keepdims=True)
    return yc * lax.rsqrt(var + LN_EPS) * g + b


def _rms_norm(y, g):
    return y * lax.rsqrt(jnp.mean(y * y, axis=-1, keepdims=True) + RMS_EPS) * g


def _pack_rows(t):
    n = t.shape[1] // 2
    bits = pltpu.bitcast(t.astype(BF16).astype(F32), jnp.uint32)
    return (bits[:, :n] >> 16) | bits[:, n:]


def _unpack_rows(w):
    lo = pltpu.bitcast(w << 16, F32).astype(BF16)
    hi = pltpu.bitcast(w & jnp.uint32(0xFFFF0000), F32).astype(BF16)
    return jnp.concatenate([lo, hi], axis=1)


def _mod_kernel(c_ref, w_ref, b_ref, o_ref):
    c = c_ref[...]
    o_ref[0] = jnp.dot(_silu(c), w_ref[0], precision=HIGHEST, preferred_element_type=F32) + b_ref[0]


def _modulation(cvec, w_mod, b_mod):
    depth, d, n = w_mod.shape
    tn = 1536
    return pl.pallas_call(
        _mod_kernel,
        grid=(depth, n // tn),
        in_specs=[
            pl.BlockSpec((8, d), lambda l, j: (0, 0)),
            pl.BlockSpec((1, d, tn), lambda l, j: (l, 0, j)),
            pl.BlockSpec((1, 1, tn), lambda l, j: (l, 0, j)),
        ],
        out_specs=pl.BlockSpec((1, 8, tn), lambda l, j: (l, 0, j)),
        out_shape=jax.ShapeDtypeStruct((depth, 8, n), F32),
        compiler_params=_cparams(("arbitrary", "arbitrary")),
        name="modulation",
    )(cvec, w_mod, b_mod.reshape(depth, 1, n))


def _proj_in_kernel(x_ref, mod_ref, ca_ref, cb_ref, w_ref, gv_ref, bv_ref, wsp_ref, bsp_ref,
                    gq_ref, wq_ref, gkv_ref, wkv_ref,
                    gm_ref, z_ref, xbc_ref, dt_ref, q_ref, k_ref, v_ref):
    tm = x_ref.shape[0]
    mod = mod_ref[0]
    h = (x_ref[...] * (1.0 + mod[1:2, :]) + mod[0:1, :]).astype(BF16)

    def proj(lo, n):
        return jnp.dot(h, w_ref[:, lo:lo + n], preferred_element_type=F32)

    uv = proj(P_GM, 2 * GM_WIDTH)
    gl = 0.5 * uv * (1.0 + lax.erf(uv * np.float32(math.sqrt(0.5))))
    u = gl[:, :GM_WIDTH]
    vn = _layer_norm(gl[:, GM_WIDTH:], gv_ref[...], bv_ref[...]).astype(BF16)
    for c in range(tm // GM_CHUNK):
        r0 = c * GM_CHUNK
        parts = [jnp.dot(wsp_ref[hh], vn[r0:r0 + GM_CHUNK, hh * LANE:(hh + 1) * LANE],
                         preferred_element_type=F32) for hh in range(GM_HEADS)]
        s = jnp.concatenate(parts, axis=1) + bsp_ref[...]
        gm_ref[r0:r0 + GM_CHUNK, :] = (u[r0:r0 + GM_CHUNK, :] * s).astype(BF16)

    z_ref[...] = proj(P_Z, SSM_INNER)
    xbc_ref[...] = proj(P_XBC, XBC_WIDTH)
    dt_ref[...] = proj(P_DT, 2 * LANE)

    ca = ca_ref[...]
    cb = cb_ref[...]
    scale = np.float32((QK_NOPE + QK_ROPE) ** -0.5)

    def rope(t):
        return t * ca + pltpu.roll(t, QK_ROPE, 1) * cb

    cqn = _rms_norm(proj(P_Q, Q_LORA), gq_ref[...]).astype(BF16)
    yq = jnp.dot(cqn, wq_ref[...], preferred_element_type=F32)
    for hh in range(MLA_HEADS):
        c0 = hh * HEAD_PAD
        q_ref[:, c0:c0 + LANE] = (yq[:, c0:c0 + LANE] * scale).astype(BF16)
        q_ref[:, c0 + LANE:c0 + HEAD_PAD] = (rope(yq[:, c0 + LANE:c0 + HEAD_PAD]) * scale).astype(BF16)

    ckvn = _rms_norm(proj(P_KV, KV_LORA), gkv_ref[...]).astype(BF16)
    krf = rope(proj(P_KR, LANE)).astype(BF16)
    ykv = jnp.dot(ckvn, wkv_ref[...], preferred_element_type=F32)
    for hh in range(MLA_HEADS):
        c0 = hh * HEAD_PAD
        k_ref[:, c0:c0 + LANE] = ykv[:, hh * LANE:(hh + 1) * LANE].astype(BF16)
        k_ref[:, c0 + LANE:c0 + HEAD_PAD] = krf
    v_ref[...] = ykv[:, MLA_HEADS * QK_NOPE:].astype(BF16)


def _proj_in(x_all, modv, ca, cb, wp, dims):
    rows = x_all.shape[0]
    tm = ROW_TILE
    nt = rows // tm
    lat_tiles = dims["seq"] // tm
    ctx_tiles = dims["ctx"] // tm
    kv_tiles = lat_tiles + ctx_tiles
    nb = dims["batch"]

    def mod_map(i):
        return (jnp.minimum(i // lat_tiles, nb), 0, 0)

    def kv_map(i):
        lat_b = i // lat_tiles
        lat_pos = lat_b * kv_tiles + ctx_tiles + i % lat_tiles
        j = i - nb * lat_tiles
        ctx_pos = (j // ctx_tiles) * kv_tiles + j % ctx_tiles
        return (jnp.where(i < nb * lat_tiles, lat_pos, ctx_pos), 0)

    row = lambda w: pl.BlockSpec((tm, w), lambda i: (i, 0))
    kv_rows = nb * (dims["seq"] + dims["ctx"])
    return pl.pallas_call(
        _proj_in_kernel,
        grid=(nt,),
        in_specs=[
            row(D_MODEL),
            pl.BlockSpec((1, 6, D_MODEL), mod_map),
            row(LANE), row(LANE),
            _resident((D_MODEL, P_WIDTH)),
            _resident((1, GM_WIDTH)), _resident((1, GM_WIDTH)),
            _resident((GM_HEADS, GM_CHUNK, GM_CHUNK)), _resident((GM_CHUNK, GM_WIDTH)),
            _resident((1, Q_LORA)), _resident((Q_LORA, MLA_HEADS * HEAD_PAD)),
            _resident((1, KV_LORA)), _resident((KV_LORA, MLA_HEADS * (QK_NOPE + V_DIM))),
        ],
        out_specs=[
            row(GM_WIDTH), row(SSM_INNER), row(XBC_WIDTH), row(2 * LANE),
            row(MLA_HEADS * HEAD_PAD),
            pl.BlockSpec((tm, MLA_HEADS * HEAD_PAD), kv_map),
            pl.BlockSpec((tm, MLA_HEADS * V_DIM), kv_map),
        ],
        out_shape=[
            jax.ShapeDtypeStruct((rows, GM_WIDTH), BF16),
            jax.ShapeDtypeStruct((rows, SSM_INNER), F32),
            jax.ShapeDtypeStruct((rows, XBC_WIDTH), F32),
            jax.ShapeDtypeStruct((rows, 2 * LANE), F32),
            jax.ShapeDtypeStruct((rows, MLA_HEADS * HEAD_PAD), BF16),
            jax.ShapeDtypeStruct((kv_rows, MLA_HEADS * HEAD_PAD), BF16),
            jax.ShapeDtypeStruct((kv_rows, MLA_HEADS * V_DIM), BF16),
        ],
        compiler_params=_cparams(("arbitrary",)),
        name="proj_in",
    )(x_all, modv, ca, cb, wp["w_in"], wp["g_v"], wp["b_v"], wp["w_sp"], wp["b_sp"],
      wp["g_q"], wp["w_q"], wp["g_kv"], wp["w_kv"])


def _ssd_kernel(backward, n_ctx, n_lat, *refs):
    if backward:
        (xp_ref, xc_ref, xn_ref, dt_ref, cw_ref, cbias_ref, dtb_ref, alog_ref, e_ref,
         yf_ref, z_ref, skip_ref, gssm_ref, out_ref, h_scr) = refs
    else:
        (xp_ref, xc_ref, xn_ref, dt_ref, cw_ref, cbias_ref, dtb_ref, alog_ref, e_ref,
         out_ref, h_scr) = refs
    q = SSM_CHUNK
    s = pl.program_id(1)
    is_ctx = s < n_ctx
    if backward:
        cc = jnp.where(is_ctx, n_ctx - 1 - s, n_lat - 1 - (s - n_ctx))
    else:
        cc = jnp.where(is_ctx, s, s - n_ctx)
    ncc = jnp.where(is_ctx, n_ctx, n_lat)
    at_start = cc == 0
    at_end = cc == ncc - 1

    @pl.when(s == 0)
    def _():
        h_scr[...] = jnp.zeros_like(h_scr)

    prev = jnp.where(at_start, 0.0, xp_ref[...])
    nxt = jnp.where(at_end, 0.0, xn_ref[...])
    ext = jnp.concatenate([prev, xc_ref[...], nxt], axis=0)
    acc = jnp.broadcast_to(cbias_ref[...], (q, XBC_WIDTH))
    base = 8 - SSM_CONV // 2
    for j in range(SSM_CONV):
        acc = acc + cw_ref[j:j + 1, :] * ext[base + j:base + j + q, :]
    u = _silu(acc)
    xc = u[:, :SSM_INNER]

    lane = lax.broadcasted_iota(jnp.int32, (q, LANE), 1)
    dtr = dt_ref[...] + dtb_ref[...]
    dtv = jnp.maximum(dtr, 0.0) + jnp.log1p(jnp.exp(-jnp.abs(dtr)))
    dtv = jnp.where(lane < SSM_HEADS, dtv, 0.0)
    da = dtv * (-jnp.exp(alog_ref[...]))

    ri = lax.broadcasted_iota(jnp.int32, (q, q), 0)
    ci = lax.broadcasted_iota(jnp.int32, (q, q), 1)
    mask = (ci >= ri) if backward else (ci <= ri)
    tri = jnp.where(mask, 1.0, 0.0).astype(F32)
    acs = jnp.dot(tri, da, precision=HIGHEST, preferred_element_type=F32)
    acs_t = acs.T
    last = 0 if backward else q - 1
    acs_last = acs[last:last + 1, :]
    small = jnp.concatenate([dtv, jnp.exp(acs_last - acs), jnp.exp(acs)], axis=0)
    wide = jnp.dot(small, e_ref[...], precision=HIGHEST, preferred_element_type=F32)
    dt_w = wide[:q]
    decay_w = wide[q:2 * q]
    eacs_w = wide[2 * q:]
    elast_w = eacs_w[last:last + 1, :]

    xdt = xc * dt_w
    xw_b = (xdt * decay_w).astype(BF16)
    h_old = h_scr[...]
    h_b = h_old.astype(BF16)

    lane_w = lax.broadcasted_iota(jnp.int32, (q, SSM_INNER), 1) // SSM_HEAD_DIM
    w_parts = []
    x_parts = []
    inter = []
    upd = []
    per_g = SSM_HEADS // SSM_GROUPS
    gw = per_g * SSM_HEAD_DIM
    for g in range(SSM_GROUPS):
        b_g = u[:, SSM_INNER + g * SSM_STATE:SSM_INNER + (g + 1) * SSM_STATE].astype(BF16)
        o_c = SSM_INNER + SSM_GROUPS * SSM_STATE
        c_g = u[:, o_c + g * SSM_STATE:o_c + (g + 1) * SSM_STATE].astype(BF16)
        cbm = lax.dot_general(c_g, b_g, (((1,), (1,)), ((), ())), preferred_element_type=F32)
        for hh in range(per_g):
            hd = g * per_g + hh
            seg = acs[:, hd:hd + 1] - acs_t[hd:hd + 1, :]
            lmat = jnp.exp(jnp.where(mask, seg, -jnp.inf))
            w_parts.append((cbm * lmat).astype(BF16))
            x_parts.append(jnp.where(lane_w == hd, xdt, 0.0).astype(BF16))
        inter.append(jnp.dot(c_g, h_b[:, g * gw:(g + 1) * gw], preferred_element_type=F32))
        upd.append(lax.dot_general(b_g, xw_b[:, g * gw:(g + 1) * gw], (((0,), (0,)), ((), ())),
                                   preferred_element_type=F32))
    y = jnp.dot(jnp.concatenate(w_parts, axis=1), jnp.concatenate(x_parts, axis=0),
                preferred_element_type=F32)
    y = y + jnp.concatenate(inter, axis=1) * eacs_w
    h_scr[...] = h_old * elast_w + jnp.concatenate(upd, axis=1)

    if backward:
        tot = yf_ref[...] + y + skip_ref[...] * xc
        yz = tot * _silu(z_ref[...])
        half = SSM_INNER // SSM_GROUPS
        outs = []
        for g in range(SSM_GROUPS):
            grp = yz[:, g * half:(g + 1) * half]
            outs.append(grp * lax.rsqrt(jnp.mean(grp * grp, axis=-1, keepdims=True) + RMS_EPS))
        out_ref[...] = (jnp.concatenate(outs, axis=1) * gssm_ref[...]).astype(BF16)
    else:
        out_ref[...] = y


def _ssd_sweep(backward, xbc, dtc, wp, dims, yf=None, z=None):
    nb = dims["batch"]
    n_lat = dims["seq"] // SSM_CHUNK
    n_ctx = dims["ctx"] // SSM_CHUNK
    ctx_base = nb * n_lat
    rows = xbc.shape[0]
    sub = SSM_CHUNK // 8

    def blk(b, s):
        is_ctx = s < n_ctx
        if backward:
            cc = jnp.where(is_ctx, n_ctx - 1 - s, n_lat - 1 - (s - n_ctx))
        else:
            cc = jnp.where(is_ctx, s, s - n_ctx)
        return jnp.where(is_ctx, ctx_base + b * n_ctx + cc, b * n_lat + cc)

    cur = lambda w: pl.BlockSpec((SSM_CHUNK, w), lambda b, s: (blk(b, s), 0))
    d = 1 if backward else 0
    in_specs = [
        pl.BlockSpec((8, XBC_WIDTH), lambda b, s: (jnp.maximum(blk(b, s) * sub - 1, 0), 0)),
        cur(XBC_WIDTH),
        pl.BlockSpec((8, XBC_WIDTH), lambda b, s: (jnp.minimum((blk(b, s) + 1) * sub, rows // 8 - 1), 0)),
        pl.BlockSpec((SSM_CHUNK, LANE), lambda b, s: (blk(b, s), d)),
        _resident((8, XBC_WIDTH)), _resident((1, XBC_WIDTH)),
        _resident((1, LANE)), _resident((1, LANE)), _resident((LANE, SSM_INNER)),
    ]
    args = [xbc, xbc, xbc, dtc, wp["conv_w"], wp["conv_b"], wp["dt_bias"][d], wp["a_log"][d], wp["expand"]]
    if backward:
        in_specs += [cur(SSM_INNER), cur(SSM_INNER), _resident((1, SSM_INNER)), _resident((1, SSM_INNER))]
        args += [yf, z, wp["d_skip"], wp["g_ssm"]]
    return pl.pallas_call(
        functools.partial(_ssd_kernel, backward, n_ctx, n_lat),
        grid=(nb, n_ctx + n_lat),
        in_specs=in_specs,
        out_specs=cur(SSM_INNER),
        out_shape=jax.ShapeDtypeStruct((rows, SSM_INNER), BF16 if backward else F32),
        scratch_shapes=[pltpu.VMEM((SSM_STATE, SSM_INNER), F32)],
        compiler_params=_cparams(("arbitrary", "arbitrary")),
        name="ssd_bwd" if backward else "ssd_fwd",
    )(*args)


def _attn_kernel(nk, tk, q_ref, k_ref, v_ref, o_ref, m_scr, l_scr, acc_scr):
    q = q_ref[...]
    m_scr[...] = jnp.full_like(m_scr, -jnp.inf)
    l_scr[...] = jnp.zeros_like(l_scr)
    acc_scr[...] = jnp.zeros_like(acc_scr)

    def body(i, carry):
        off = pl.multiple_of(i * tk, tk)
        k = k_ref[pl.ds(off, tk), :]
        v = v_ref[pl.ds(off, tk), :]
        s = lax.dot_general(q, k, (((1,), (1,)), ((), ())), preferred_element_type=F32)
        m_prev = m_scr[...]
        m_new = jnp.maximum(m_prev, jnp.max(s, axis=1, keepdims=True))
        alpha = jnp.exp(m_prev - m_new)
        p = jnp.exp(s - jnp.tile(m_new, (1, tk // LANE)))
        l_scr[...] = alpha * l_scr[...] + jnp.sum(p, axis=1, keepdims=True)
        acc_scr[...] = alpha * acc_scr[...] + jnp.dot(p.astype(BF16), v, preferred_element_type=F32)
        m_scr[...] = m_new
        return carry

    lax.fori_loop(0, nk, body, 0)
    o_ref[...] = (acc_scr[...] / l_scr[...]).astype(BF16)


def _attention(q, k, v, nb, lq, lk, q_row0, kv_stride, tq, tk):
    nq = lq // tq
    q0 = q_row0 // tq
    assert kv_stride % lk == 0 and lk % tk == 0 and q_row0 % tq == 0
    kvb = kv_stride // lk
    return pl.pallas_call(
        functools.partial(_attn_kernel, lk // tk, tk),
        grid=(nb, MLA_HEADS, nq),
        in_specs=[
            pl.BlockSpec((tq, HEAD_PAD), lambda b, h, i: (q0 + b * nq + i, h)),
            pl.BlockSpec((lk, HEAD_PAD), lambda b, h, i: (b * kvb, h)),
            pl.BlockSpec((lk, V_DIM), lambda b, h, i: (b * kvb, h)),
        ],
        out_specs=pl.BlockSpec((tq, V_DIM), lambda b, h, i: (b * nq + i, h)),
        out_shape=jax.ShapeDtypeStruct((nb * lq, MLA_HEADS * V_DIM), BF16),
        scratch_shapes=[pltpu.VMEM((tq, LANE), F32), pltpu.VMEM((tq, LANE), F32), pltpu.VMEM((tq, V_DIM), F32)],
        compiler_params=_cparams(("arbitrary", "arbitrary", "arbitrary")),
        name="mla_attention",
    )(q, k, v)


def _out_proj_kernel(n_lat_tiles, gm_ref, ssm_ref, att_ref, attc_ref, x_ref, mod_ref, w_ref, g_ref, b_ref, wr_ref,
                     x1_ref, tok_ref, lg_ref):
    mod = mod_ref[0]
    att = att_ref[...]
    if attc_ref is not None:
        att = jnp.where(pl.program_id(0) >= n_lat_tiles, attc_ref[...], att)
    mixed = jnp.concatenate([gm_ref[...], ssm_ref[...], att], axis=1)
    mix = jnp.dot(mixed, w_ref[...], preferred_element_type=F32)
    x1 = _layer_norm(DN_ALPHA * x_ref[...] + mod[2:3, :] * mix, g_ref[...], b_ref[...])
    x1_ref[...] = x1
    tok = x1 * (1.0 + mod[4:5, :]) + mod[3:4, :]
    tok_ref[...] = _pack_rows(tok)
    lg_ref[...] = lax.dot_general(wr_ref[...], tok, (((1,), (1,)), ((), ())),
                                  precision=HIGHEST, preferred_element_type=F32)


def _out_proj_lat_kernel(n_lat_tiles, gm_ref, ssm_ref, att_ref, *rest):
    _out_proj_kernel(n_lat_tiles, gm_ref, ssm_ref, att_ref, None, *rest)


def _out_proj(gm, ssm, att, att_ctx, x_all, modv, wp, dims, rows):
    tm = ROW_TILE
    lat_tiles = dims["seq"] // tm
    nb = dims["batch"]
    n_lat = nb * lat_tiles
    row = lambda w: pl.BlockSpec((tm, w), lambda i: (i, 0))
    att_specs = [pl.BlockSpec((tm, MLA_HEADS * V_DIM), lambda i: (jnp.minimum(i, n_lat - 1), 0))]
    att_args = [att]
    body = _out_proj_lat_kernel
    if att_ctx is not None:
        att_specs.append(pl.BlockSpec((tm, MLA_HEADS * V_DIM), lambda i: (jnp.maximum(i - n_lat, 0), 0)))
        att_args.append(att_ctx)
        body = _out_proj_kernel
    return pl.pallas_call(
        functools.partial(body, n_lat),
        grid=(rows // tm,),
        in_specs=[
            row(GM_WIDTH), row(SSM_INNER), *att_specs, row(D_MODEL),
            pl.BlockSpec((1, 6, D_MODEL), lambda i: (jnp.minimum(i // lat_tiles, nb), 0, 0)),
            _resident((D_MODEL, D_MODEL)), _resident((1, D_MODEL)), _resident((1, D_MODEL)),
            _resident((N_EXPERTS, D_MODEL)),
        ],
        out_specs=[row(D_MODEL), row(D_MODEL // 2), pl.BlockSpec((N_EXPERTS, tm), lambda i: (0, i))],
        out_shape=[
            jax.ShapeDtypeStruct((rows, D_MODEL), F32),
            jax.ShapeDtypeStruct((rows, D_MODEL // 2), jnp.uint32),
            jax.ShapeDtypeStruct((N_EXPERTS, rows), F32),
        ],
        compiler_params=_cparams(("arbitrary",)),
        name="out_proj",
    )(gm, ssm, *att_args, x_all, modv, wp["w_out"], wp["ln1_g"], wp["ln1_b"], wp["w_router_t"])


def _gate_kernel(lg_ref, b_ref, eidx_ref, w_ref, pos_ref, cnt_ref, carry_scr):
    per = N_EXPERTS // N_EXPERT_GROUPS
    tn = lg_ref.shape[1]
    sc = jax.nn.sigmoid(lg_ref[...])
    bi = sc + b_ref[...]
    neg = np.float32(-np.inf)
    xs = [bi[j * N_EXPERT_GROUPS:(j + 1) * N_EXPERT_GROUPS, :] for j in range(per)]
    ss = [sc[j * N_EXPERT_GROUPS:(j + 1) * N_EXPERT_GROUPS, :] for j in range(per)]
    m1 = functools.reduce(jnp.maximum, xs)
    first = functools.reduce(jnp.minimum, [jnp.where(xs[j] == m1, j, per) for j in range(per)])
    m2 = functools.reduce(jnp.maximum, [jnp.where(first == j, neg, xs[j]) for j in range(per)])
    gscore = m1 + m2
    gi = lax.broadcasted_iota(jnp.int32, (N_EXPERT_GROUPS, tn), 0)
    cur = gscore
    gsel = jnp.zeros((N_EXPERT_GROUPS, tn), jnp.int32)
    for _ in range(TOPK_GROUPS):
        mx = jnp.max(cur, axis=0, keepdims=True)
        fi = jnp.min(jnp.where(cur == mx, gi, N_EXPERT_GROUPS), axis=0, keepdims=True)
        pick = gi == fi
        gsel = jnp.where(pick, 1, gsel)
        cur = jnp.where(pick, neg, cur)
    cur = [jnp.where(gsel > 0, xs[j], neg) for j in range(per)]
    eid = [gi * per + j for j in range(per)]
    sel = [jnp.zeros((N_EXPERT_GROUPS, tn), F32) for _ in range(per)]
    picked = []
    for _ in range(TOP_K):
        mx = jnp.max(functools.reduce(jnp.maximum, cur), axis=0, keepdims=True)
        cand = functools.reduce(jnp.minimum, [jnp.where(cur[j] == mx, eid[j], N_EXPERTS) for j in range(per)])
        fi = jnp.min(cand, axis=0, keepdims=True)
        wk = jnp.zeros((N_EXPERT_GROUPS, tn), F32)
        for j in range(per):
            pick = eid[j] == fi
            sel[j] = jnp.where(pick, 1.0, sel[j])
            cur[j] = jnp.where(pick, neg, cur[j])
            wk = wk + jnp.where(pick, ss[j], 0.0)
        picked.append((fi, jnp.sum(wk, axis=0, keepdims=True)))
    tot = functools.reduce(jnp.add, [wk for _, wk in picked])

    @pl.when(pl.program_id(0) == 0)
    def _():
        carry_scr[...] = jnp.zeros_like(carry_scr)

    sel_all = jnp.concatenate(sel, axis=0)
    ri = lax.broadcasted_iota(jnp.int32, (tn, tn), 0)
    ci = lax.broadcasted_iota(jnp.int32, (tn, tn), 1)
    before = jnp.where(ri < ci, 1.0, 0.0).astype(BF16)
    prefix = jnp.dot(sel_all.astype(BF16), before, preferred_element_type=F32)
    carry = carry_scr[...]
    rank = prefix + carry
    carry_new = carry + prefix[:, tn - 1:tn] + sel_all[:, tn - 1:tn]
    carry_scr[...] = carry_new
    cnt_ref[...] = jnp.broadcast_to(carry_new, cnt_ref.shape).astype(jnp.int32)
    for k, (fi, wk) in enumerate(picked):
        pk = jnp.zeros((N_EXPERT_GROUPS, tn), F32)
        for j in range(per):
            pk = pk + jnp.where(eid[j] == fi, rank[j * N_EXPERT_GROUPS:(j + 1) * N_EXPERT_GROUPS, :], 0.0)
        eidx_ref[k:k + 1, :] = fi
        w_ref[k:k + 1, :] = wk / tot * ROUTED_SCALE
        pos_ref[k:k + 1, :] = jnp.sum(pk, axis=0, keepdims=True).astype(jnp.int32)


def _gate(logits_t, b_router_col):
    t = logits_t.shape[1]
    tn = 512
    tile = pl.BlockSpec((TOP_K, tn), lambda i: (0, i))
    return pl.pallas_call(
        _gate_kernel,
        grid=(t // tn,),
        in_specs=[pl.BlockSpec((N_EXPERTS, tn), lambda i: (0, i)), _resident((N_EXPERTS, 1))],
        out_specs=[tile, tile, tile, pl.BlockSpec((N_EXPERTS, LANE), lambda i: (0, 0))],
        out_shape=[
            jax.ShapeDtypeStruct((TOP_K, t), jnp.int32),
            jax.ShapeDtypeStruct((TOP_K, t), F32),
            jax.ShapeDtypeStruct((TOP_K, t), jnp.int32),
            jax.ShapeDtypeStruct((N_EXPERTS, LANE), jnp.int32),
        ],
        scratch_shapes=[pltpu.VMEM((N_EXPERTS, 1), F32)],
        compiler_params=_cparams(("arbitrary",)),
        name="gate_topk",
    )(logits_t, b_router_col)


DISPATCH_TILE = 512


def _dispatch_kernel(padlo_ref, padn_ref, nu_ref, slot_ref, tok_hbm, xs_hbm, zero_buf, sem, zsem):
    i = pl.program_id(0)
    td = slot_ref.shape[1]
    base = i * td

    def row_copy(r, k):
        return pltpu.make_async_copy(tok_hbm.at[pl.ds(base + r, 1)], xs_hbm.at[pl.ds(slot_ref[k, r], 1)], sem)

    def issue(r, carry):
        for k in range(TOP_K):
            row_copy(r, k).start()
        return carry

    lax.fori_loop(0, td, issue, 0)

    @pl.when(i == 0)
    def _():
        zero_buf[...] = jnp.zeros_like(zero_buf)

        def fill(e, carry):
            lo = padlo_ref[e]
            n = padn_ref[e]
            head = jnp.minimum((-lo) & 7, n)

            for s in range(7):
                @pl.when(s < head)
                def _(s=s):
                    cp = pltpu.make_async_copy(zero_buf.at[pl.ds(0, 1)], xs_hbm.at[pl.ds(lo + s, 1)], zsem)
                    cp.start()
                    cp.wait()

            rest = n - head
            off = lo + head
            size = EXPERT_BLOCK // 2
            while size >= 8:
                hit = (rest & size) != 0

                @pl.when(hit)
                def _(off=off, size=size):
                    dst = xs_hbm.at[pl.ds(pl.multiple_of(off, 8), size)]
                    cp = pltpu.make_async_copy(zero_buf.at[pl.ds(0, size)], dst, zsem)
                    cp.start()
                    cp.wait()

                off = off + jnp.where(hit, size, 0)
                size //= 2
            return carry

        lax.fori_loop(0, N_EXPERTS, fill, 0)

        zrows = zero_buf.shape[0]

        def fill_tail(b, carry):
            for h in range(EXPERT_BLOCK // zrows):
                row0 = pl.multiple_of(b * EXPERT_BLOCK + h * zrows, 8)
                cp = pltpu.make_async_copy(zero_buf, xs_hbm.at[pl.ds(row0, zrows)], zsem)
                cp.start()
                cp.wait()
            return carry

        lax.fori_loop(nu_ref[0], xs_hbm.shape[0] // EXPERT_BLOCK, fill_tail, 0)

    for _ in range(TOP_K):
        pltpu.make_async_copy(tok_hbm.at[pl.ds(0, td)], xs_hbm.at[pl.ds(0, td)], sem).wait()


def _dispatch_rows(slot, pad_lo, pad_n, n_used, tok, cap):
    t = tok.shape[0]
    td = DISPATCH_TILE
    return pl.pallas_call(
        _dispatch_kernel,
        grid_spec=pltpu.PrefetchScalarGridSpec(
            num_scalar_prefetch=3,
            grid=(t // td,),
            in_specs=[
                pl.BlockSpec((TOP_K, td), lambda i, lo, n, nu: (0, i), memory_space=pltpu.SMEM),
                pl.BlockSpec(memory_space=pl.ANY),
            ],
            out_specs=pl.BlockSpec(memory_space=pl.ANY),
            scratch_shapes=[pltpu.VMEM((EXPERT_BLOCK // 2, D_MODEL // 2), jnp.uint32),
                            pltpu.SemaphoreType.DMA, pltpu.SemaphoreType.DMA],
        ),
        out_shape=jax.ShapeDtypeStruct((cap, D_MODEL // 2), jnp.uint32),
        compiler_params=_cparams(("arbitrary",)),
        name="dispatch_rows",
    )(pad_lo, pad_n, n_used, slot, tok)


def _experts_kernel(be_ref, nu_ref, x_ref, wg_ref, wu_ref, wd_ref, y_ref, wg_s, wu_s, wd_s):
    j = pl.program_id(0)
    prev = be_ref[jnp.maximum(j - 1, 0)]
    fresh = jnp.logical_or(j == 0, be_ref[j] != prev)

    @pl.when(fresh)
    def _():
        wg_s[...] = wg_ref[0].astype(BF16)
        wu_s[...] = wu_ref[0].astype(BF16)
        wd_s[...] = wd_ref[0].astype(BF16)

    @pl.when(j < nu_ref[0])
    def _():
        x = _unpack_rows(x_ref[...])
        g = jnp.dot(x, wg_s[...], preferred_element_type=F32)
        u = jnp.dot(x, wu_s[...], preferred_element_type=F32)
        hb = (_silu(g) * u).astype(BF16)
        y_ref[...] = jnp.dot(hb, wd_s[...], preferred_element_type=F32)

    @pl.when(j >= nu_ref[0])
    def _():
        y_ref[...] = jnp.zeros_like(y_ref)


def _experts(layer, block_e, n_used, x_sorted, w_gate, w_up, w_down):
    cap = x_sorted.shape[0]
    blk = EXPERT_BLOCK
    return pl.pallas_call(
        _experts_kernel,
        grid_spec=pltpu.PrefetchScalarGridSpec(
            num_scalar_prefetch=2,
            grid=(cap // blk,),
            in_specs=[
                pl.BlockSpec((blk, D_MODEL // 2), lambda j, be, nu: (jnp.minimum(j, nu[0] - 1), 0)),
                pl.BlockSpec((None, 1, D_MODEL, EXPERT_FF), lambda j, be, nu: (layer, be[j], 0, 0)),
                pl.BlockSpec((None, 1, D_MODEL, EXPERT_FF), lambda j, be, nu: (layer, be[j], 0, 0)),
                pl.BlockSpec((None, 1, EXPERT_FF, D_MODEL), lambda j, be, nu: (layer, be[j], 0, 0)),
            ],
            out_specs=pl.BlockSpec((blk, D_MODEL), lambda j, be, nu: (j, 0)),
            scratch_shapes=[pltpu.VMEM((D_MODEL, EXPERT_FF), BF16), pltpu.VMEM((D_MODEL, EXPERT_FF), BF16),
                            pltpu.VMEM((EXPERT_FF, D_MODEL), BF16)],
        ),
        out_shape=jax.ShapeDtypeStruct((cap, D_MODEL), F32),
        compiler_params=_cparams(("arbitrary",)),
        name="routed_experts",
    )(block_e, n_used, x_sorted, w_gate, w_up, w_down)


COMBINE_TILE = 128


def _final_kernel(slot_ref, slot_next_ref, w8_ref, tok_ref, x1_ref, mod_ref, wgu_ref, wd_ref, g_ref, b_ref,
                  y_hbm, o_ref, ybuf, sems):
    i = pl.program_id(0)
    n = pl.num_programs(0)
    tf = tok_ref.shape[0]
    cur = i % 2

    def gather(slots, buf):
        def issue(r, carry):
            for k in range(TOP_K):
                pltpu.make_async_copy(y_hbm.at[pl.ds(slots[k, r], 1)], ybuf.at[buf, k, pl.ds(r, 1)],
                                      sems.at[buf]).start()
            return carry

        lax.fori_loop(0, tf, issue, 0)

    @pl.when(i == 0)
    def _():
        gather(slot_ref, 0)

    @pl.when(i + 1 < n)
    def _():
        gather(slot_next_ref, 1 - cur)

    mod = mod_ref[0]
    gu = jnp.dot(_unpack_rows(tok_ref[...]), wgu_ref[...], preferred_element_type=F32)
    hb = (_silu(gu[:, :SHARED_FF]) * gu[:, SHARED_FF:]).astype(BF16)
    f = jnp.dot(hb, wd_ref[...], preferred_element_type=F32)

    for k in range(TOP_K):
        pltpu.make_async_copy(y_hbm.at[pl.ds(0, tf)], ybuf.at[cur, k], sems.at[cur]).wait()
    w8 = w8_ref[...]
    for k in range(TOP_K):
        f = f + w8[:, k:k + 1] * ybuf[cur, k]
    o_ref[...] = _layer_norm(DN_ALPHA * x1_ref[...] + mod[5:6, :] * f, g_ref[...], b_ref[...])


def _final(slot, w8_t, tok, y_sorted, x1, modv, wp, dims, rows):
    tf = COMBINE_TILE
    nt = rows // tf
    lat_tiles = dims["seq"] // tf
    nb = dims["batch"]
    row = lambda w: pl.BlockSpec((tf, w), lambda i: (i, 0))
    return pl.pallas_call(
        _final_kernel,
        grid=(nt,),
        in_specs=[
            pl.BlockSpec((TOP_K, tf), lambda i: (0, i), memory_space=pltpu.SMEM),
            pl.BlockSpec((TOP_K, tf), lambda i: (0, jnp.minimum(i + 1, nt - 1)), memory_space=pltpu.SMEM),
            row(TOP_K), row(D_MODEL // 2), row(D_MODEL),
            pl.BlockSpec((1, 6, D_MODEL), lambda i: (jnp.minimum(i // lat_tiles, nb), 0, 0)),
            _resident((D_MODEL, 2 * SHARED_FF)), _resident((SHARED_FF, D_MODEL)),
            _resident((1, D_MODEL)), _resident((1, D_MODEL)),
            pl.BlockSpec(memory_space=pl.ANY),
        ],
        out_specs=row(D_MODEL),
        out_shape=jax.ShapeDtypeStruct((rows, D_MODEL), F32),
        scratch_shapes=[pltpu.VMEM((2, TOP_K, tf, D_MODEL), F32), pltpu.SemaphoreType.DMA((2,))],
        compiler_params=_cparams(("arbitrary",)),
        name="combine_shared_ln2",
    )(slot, slot, w8_t, tok, x1, modv, wp["w_sh_gu"], wp["w_sh_down"], wp["ln2_g"], wp["ln2_b"], y_sorted)


def _rot_half_cols():
    src = np.zeros((QK_ROPE,), np.int32)
    sign = np.zeros((QK_ROPE,), np.float32)
    quarter = QK_ROPE // 4
    for dcol in range(QK_ROPE):
        part, i = divmod(dcol, 2 * quarter)
        half, kk = divmod(i, quarter)
        src[dcol] = part * 2 * quarter + (quarter + kk if half == 0 else kk)
        sign[dcol] = -1.0 if half == 0 else 1.0
    return src, sign


def _expert_major(a, axis):
    per = N_EXPERTS // N_EXPERT_GROUPS
    shp = a.shape
    a = a.reshape(shp[:axis] + (N_EXPERT_GROUPS, per) + shp[axis + 1:])
    a = jnp.swapaxes(a, axis, axis + 1)
    return a.reshape(shp)


def _prep_layer(l, p):
    src, sign = _rot_half_cols()
    w_in = p["w_in"][l]
    col = lambda lo, n: w_in[:, lo:lo + n]
    zpad = jnp.zeros((D_MODEL, LANE - SSM_HEADS), F32)
    w_kr = col(R_KR, QK_ROPE)
    w_in_p = jnp.concatenate([
        col(R_GM, 2 * GM_WIDTH), col(R_Q, Q_LORA), col(R_Z, SSM_INNER), col(R_XBC, XBC_WIDTH),
        col(R_DT, SSM_HEADS), zpad, col(R_DT + SSM_HEADS, SSM_HEADS), zpad,
        col(R_KV, KV_LORA), w_kr, w_kr[:, src] * sign,
    ], axis=1).astype(BF16)
    wq = p["w_q_b"][l].reshape(Q_LORA, MLA_HEADS, QK_NOPE + QK_ROPE)
    wq_rope = wq[:, :, QK_NOPE:]
    w_q = jnp.concatenate([wq[:, :, :QK_NOPE], wq_rope, wq_rope[:, :, src] * sign], axis=2)
    w_q = w_q.reshape(Q_LORA, MLA_HEADS * HEAD_PAD).astype(BF16)
    wkv = p["w_kv_b"][l].reshape(KV_LORA, MLA_HEADS, QK_NOPE + V_DIM)
    w_kv = jnp.concatenate([wkv[:, :, :QK_NOPE].reshape(KV_LORA, -1), wkv[:, :, QK_NOPE:].reshape(KV_LORA, -1)],
                           axis=1).astype(BF16)
    lane_pad = lambda a: jnp.pad(a, ((0, 0), (0, LANE - a.shape[1])))
    expand = jnp.repeat(jnp.eye(LANE, SSM_HEADS, dtype=F32), SSM_HEAD_DIM, axis=1)
    r2 = lambda a: a.reshape(1, -1)
    return dict(
        w_in=w_in_p, g_v=r2(p["g_v"][l]), b_v=r2(p["b_v"][l]), w_sp=p["w_sp"][l].astype(BF16),
        b_sp=jnp.repeat(p["b_sp"][l].T, GM_CHUNK, axis=1),
        g_q=r2(p["g_q"][l]), w_q=w_q, g_kv=r2(p["g_kv"][l]), w_kv=w_kv,
        conv_w=jnp.pad(p["conv_w"][l], ((0, 8 - SSM_CONV), (0, 0))), conv_b=r2(p["conv_b"][l]),
        dt_bias=lane_pad(p["dt_bias"][l])[:, None, :], a_log=lane_pad(p["a_log"][l])[:, None, :],
        expand=expand, d_skip=r2(jnp.repeat(p["d_skip"][l], SSM_HEAD_DIM)), g_ssm=r2(p["g_ssm"][l]),
        w_out=p["w_out"][l].astype(BF16), ln1_g=r2(p["ln1_g"][l]), ln1_b=r2(p["ln1_b"][l]),
        ln2_g=r2(p["ln2_g"][l]), ln2_b=r2(p["ln2_b"][l]),
        w_router_t=_expert_major(p["w_router"][l].T, 0),
        b_router=_expert_major(p["b_router"][l], 0).reshape(N_EXPERTS, 1),
        w_sh_gu=jnp.concatenate([p["w_sh_gate"][l], p["w_sh_up"][l]], axis=1).astype(BF16),
        w_sh_down=p["w_sh_down"][l].astype(BF16),
    )


def _rope_tables(nb, seq, nctx):
    t = jnp.arange(seq)
    half = QK_ROPE // 2
    inv = 1.0 / (ROPE_BASE ** (jnp.arange(0, half, 2, dtype=F32) / half))
    ang_r = (t // GRID_W)[:, None] * inv
    ang_c = (t % GRID_W)[:, None] * inv
    ang = jnp.concatenate([ang_r, ang_r, ang_c, ang_c], axis=1)
    zero = jnp.zeros((seq, LANE - QK_ROPE), F32)
    ca_lat = jnp.concatenate([jnp.cos(ang), zero], axis=1)
    cb_lat = jnp.concatenate([jnp.sin(ang), zero], axis=1)
    ca_ctx = jnp.concatenate([jnp.ones((nb * nctx, QK_ROPE), F32), jnp.zeros((nb * nctx, LANE - QK_ROPE), F32)], axis=1)
    ca = jnp.concatenate([jnp.tile(ca_lat, (nb, 1)), ca_ctx], axis=0)
    cb = jnp.concatenate([jnp.tile(cb_lat, (nb, 1)), jnp.zeros((nb * nctx, LANE), F32)], axis=0)
    return ca, cb


def _slot_layout(eidx, pos, cnt, t):
    per = N_EXPERTS // N_EXPERT_GROUPS
    blk = EXPERT_BLOCK
    counts = cnt[:, 0].reshape(per, N_EXPERT_GROUPS).T.reshape(N_EXPERTS)
    padded = (counts + blk - 1) // blk * blk
    pend = jnp.cumsum(padded)
    pstart = pend - padded
    n_blocks = (t * TOP_K + N_EXPERTS * (blk - 1) + blk - 1) // blk
    block_e = jnp.minimum(jnp.searchsorted(pend, jnp.arange(n_blocks) * blk, side="right"), N_EXPERTS - 1)
    n_used = (pend[-1] // blk).reshape(1)
    slot = (pstart[eidx] + pos).astype(jnp.int32)
    return (slot, block_e.astype(jnp.int32), n_used.astype(jnp.int32), (pstart + counts).astype(jnp.int32),
            (padded - counts).astype(jnp.int32), n_blocks * blk)


def kernel(x, c, ctx, c_ctx, w_mod, b_mod, w_in, g_q, w_q_b, g_kv, w_kv_b, conv_w, conv_b, a_log, dt_bias, d_skip, g_ssm, g_v, b_v, w_sp, b_sp, w_out, ln1_g, ln1_b, ln2_g, ln2_b, w_router, b_router, w_e_gate, w_e_up, w_e_down, w_sh_gate, w_sh_up, w_sh_down):
    p = dict(w_in=w_in, g_q=g_q, w_q_b=w_q_b, g_kv=g_kv, w_kv_b=w_kv_b, conv_w=conv_w, conv_b=conv_b,
             a_log=a_log, dt_bias=dt_bias, d_skip=d_skip, g_ssm=g_ssm, g_v=g_v, b_v=b_v, w_sp=w_sp, b_sp=b_sp,
             w_out=w_out, ln1_g=ln1_g, ln1_b=ln1_b, ln2_g=ln2_g, ln2_b=ln2_b, w_router=w_router,
             b_router=b_router, w_sh_gate=w_sh_gate, w_sh_up=w_sh_up, w_sh_down=w_sh_down)
    nb, seq, d = x.shape
    nctx = ctx.shape[1]
    depth = w_mod.shape[0]
    assert d == D_MODEL and depth == DEPTH and nb + 1 <= 8
    assert seq % ROW_TILE == 0 and nctx % ROW_TILE == 0 and seq % GRID_W == 0
    dims = dict(batch=nb, seq=seq, ctx=nctx)
    lat_rows = nb * seq
    rows_all = lat_rows + nb * nctx
    lk = seq + nctx

    x_all = jnp.concatenate([x.reshape(lat_rows, d), ctx.reshape(nb * nctx, d)], axis=0)
    cvec = jnp.zeros((8, d), F32).at[:nb].set(c).at[nb].set(c_ctx)
    mods = _modulation(cvec, w_mod, b_mod).reshape(depth, 8, 6, d)[:, :nb + 1]
    ca, cb = _rope_tables(nb, seq, nctx)
    tq = min(512, seq)
    tk = 768 if lk % 768 == 0 else ROW_TILE

    for l in range(depth):
        last = l == depth - 1
        wp = _prep_layer(l, p)
        modv = mods[l]
        gm, z, xbc, dtc, q, k, v = _proj_in(x_all, modv, ca, cb, wp, dims)
        yf = _ssd_sweep(False, xbc, dtc, wp, dims)
        ssm = _ssd_sweep(True, xbc, dtc, wp, dims, yf=yf, z=z)
        att = _attention(q, k, v, nb, seq, lk, 0, lk, tq, tk)
        rows = lat_rows if last else rows_all
        att_ctx = None if last else _attention(q, k, v, nb, nctx, nctx, lat_rows, lk, ROW_TILE, ROW_TILE)
        x1, tok, logits_t = _out_proj(gm, ssm, att, att_ctx, x_all, modv, wp, dims, rows)
        eidx, w8, pos, cnt = _gate(logits_t, wp["b_router"])
        slot, block_e, n_used, pad_lo, pad_n, cap = _slot_layout(eidx, pos, cnt, rows)
        x_sorted = _dispatch_rows(slot, pad_lo, pad_n, n_used, tok, cap)
        y_sorted = _experts(l, block_e, n_used, x_sorted, w_e_gate, w_e_up, w_e_down)
        x_all = _final(slot, w8.T, tok, y_sorted, x1, modv, wp, dims, rows)
    return x_all.reshape(nb, seq, d)
```

```python
import functools
import math

import jax
import jax.numpy as jnp
import numpy as np
from jax import lax
from jax.experimental import pallas as pl
from jax.experimental.pallas import tpu as pltpu

F32 = jnp.float32
BF16 = jnp.bfloat16
HIGHEST = lax.Precision.HIGHEST

D_MODEL = 2048
GRID_W = 64
GM_HEADS = 4
GM_WIDTH = 512
GM_CHUNK = 128
SSM_HEADS = 8
SSM_HEAD_DIM = 64
SSM_INNER = 512
SSM_GROUPS = 2
SSM_STATE = 128
SSM_CONV = 5
SSM_CHUNK = 128
XBC_WIDTH = 1024
MLA_HEADS = 8
QK_NOPE = 128
QK_ROPE = 64
V_DIM = 128
Q_LORA = 768
KV_LORA = 256
ROPE_BASE = 10000.0
N_EXPERTS = 64
TOP_K = 8
N_EXPERT_GROUPS = 8
TOPK_GROUPS = 4
EXPERT_FF = 512
SHARED_FF = 512
ROUTED_SCALE = 2.5
LN_EPS = 1e-5
RMS_EPS = 1e-6
DEPTH = 2
DN_ALPHA = (2 * DEPTH) ** 0.25

R_GM = 0
R_Q = R_GM + 2 * GM_WIDTH
R_Z = R_Q + Q_LORA
R_XBC = R_Z + SSM_INNER
R_DT = R_XBC + XBC_WIDTH
R_KV = R_DT + 2 * SSM_HEADS
R_KR = R_KV + KV_LORA

LANE = 128
P_GM = 0
P_Q = P_GM + 2 * GM_WIDTH
P_Z = P_Q + Q_LORA
P_XBC = P_Z + SSM_INNER
P_DT = P_XBC + XBC_WIDTH
P_KV = P_DT + 2 * LANE
P_KR = P_KV + KV_LORA
P_WIDTH = P_KR + LANE

HEAD_PAD = 256
ROW_TILE = 256
EXPERT_BLOCK = 256
VMEM_LIMIT = 56 * 1024 * 1024


def _cparams(sem):
    return pltpu.CompilerParams(dimension_semantics=sem, vmem_limit_bytes=VMEM_LIMIT)


def _resident(shape):
    n = len(shape)
    return pl.BlockSpec(shape, lambda *_: (0,) * n, pipeline_mode=pl.Buffered(1))


def _silu(x):
    return x * jax.nn.sigmoid(x)


def _layer_norm(y, g, b):
    mu = jnp.mean(y, axis=-1, keepdims=True)
    yc = y - mu
    var = jnp.mean(yc * yc, axis=-1, keepdims=True)
    return yc * lax.rsqrt(var + LN_EPS) * g + b


def _rms_norm(y, g):
    return y * lax.rsqrt(jnp.mean(y * y, axis=-1, keepdims=True) + RMS_EPS) * g


def _pack_rows(t):
    n = t.shape[1] // 2
    bits = pltpu.bitcast(t.astype(BF16).astype(F32), jnp.uint32)
    return (bits[:, :n] >> 16) | bits[:, n:]


def _unpack_rows(w):
    lo = pltpu.bitcast(w << 16, F32).astype(BF16)
    hi = pltpu.bitcast(w & jnp.uint32(0xFFFF0000), F32).astype(BF16)
    return jnp.concatenate([lo, hi], axis=1)


def _mod_kernel(c_ref, w_ref, b_ref, o_ref):
    c = c_ref[...]
    o_ref[0] = jnp.dot(_silu(c), w_ref[0], precision=HIGHEST, preferred_element_type=F32) + b_ref[0]


def _modulation(cvec, w_mod, b_mod):
    depth, d, n = w_mod.shape
    tn = 1536
    return pl.pallas_call(
        _mod_kernel,
        grid=(depth, n // tn),
        in_specs=[
            pl.BlockSpec((8, d), lambda l, j: (0, 0)),
            pl.BlockSpec((1, d, tn), lambda l, j: (l, 0, j)),
            pl.BlockSpec((1, 1, tn), lambda l, j: (l, 0, j)),
        ],
        out_specs=pl.BlockSpec((1, 8, tn), lambda l, j: (l, 0, j)),
        out_shape=jax.ShapeDtypeStruct((depth, 8, n), F32),
        compiler_params=_cparams(("arbitrary", "arbitrary")),
        name="modulation",
    )(cvec, w_mod, b_mod.reshape(depth, 1, n))


def _proj_in_kernel(x_ref, mod_ref, ca_ref, cb_ref, w_ref, gv_ref, bv_ref, wsp_ref, bsp_ref,
                    gq_ref, wq_ref, gkv_ref, wk_ref, wvt_ref,
                    gm_ref, z_ref, xbc_ref, dt_ref, q_ref, k_ref, vt_ref):
    tm = x_ref.shape[0]
    mod = mod_ref[0]
    h = (x_ref[...] * (1.0 + mod[1:2, :]) + mod[0:1, :]).astype(BF16)

    def proj(lo, n):
        return jnp.dot(h, w_ref[:, lo:lo + n], preferred_element_type=F32)

    uv = proj(P_GM, 2 * GM_WIDTH)
    gl = 0.5 * uv * (1.0 + lax.erf(uv * np.float32(math.sqrt(0.5))))
    u = gl[:, :GM_WIDTH]
    vn = _layer_norm(gl[:, GM_WIDTH:], gv_ref[...], bv_ref[...]).astype(BF16)
    for c in range(tm // GM_CHUNK):
        r0 = c * GM_CHUNK
        parts = [jnp.dot(wsp_ref[hh], vn[r0:r0 + GM_CHUNK, hh * LANE:(hh + 1) * LANE],
                         preferred_element_type=F32) for hh in range(GM_HEADS)]
        s = jnp.concatenate(parts, axis=1) + bsp_ref[...]
        gm_ref[r0:r0 + GM_CHUNK, :] = (u[r0:r0 + GM_CHUNK, :] * s).astype(BF16)

    z_ref[...] = proj(P_Z, SSM_INNER)
    xbc_ref[...] = proj(P_XBC, XBC_WIDTH)
    dt_ref[...] = proj(P_DT, 2 * LANE)

    ca = ca_ref[...]
    cb = cb_ref[...]
    scale = np.float32((QK_NOPE + QK_ROPE) ** -0.5 * math.log2(math.e))

    def rope(t):
        return t * ca + pltpu.roll(t, QK_ROPE, 1) * cb

    cqn = _rms_norm(proj(P_Q, Q_LORA), gq_ref[...]).astype(BF16)
    yq = jnp.dot(cqn, wq_ref[...], preferred_element_type=F32)
    for hh in range(MLA_HEADS):
        c0 = hh * HEAD_PAD
        q_ref[:, c0:c0 + LANE] = (yq[:, c0:c0 + LANE] * scale).astype(BF16)
        q_ref[:, c0 + LANE:c0 + HEAD_PAD] = (rope(yq[:, c0 + LANE:c0 + HEAD_PAD]) * scale).astype(BF16)

    ckvn = _rms_norm(proj(P_KV, KV_LORA), gkv_ref[...]).astype(BF16)
    krf = rope(proj(P_KR, LANE)).astype(BF16)
    yk = jnp.dot(ckvn, wk_ref[...], preferred_element_type=F32)
    for hh in range(MLA_HEADS):
        c0 = hh * HEAD_PAD
        k_ref[:, c0:c0 + LANE] = yk[:, hh * LANE:(hh + 1) * LANE].astype(BF16)
        k_ref[:, c0 + LANE:c0 + HEAD_PAD] = krf
    vt_ref[...] = lax.dot_general(wvt_ref[...], ckvn, (((1,), (1,)), ((), ())),
                                  preferred_element_type=F32).astype(BF16)


def _proj_in(x_all, modv, ca, cb, wp, dims):
    rows = x_all.shape[0]
    tm = ROW_TILE
    nt = rows // tm
    lat_tiles = dims["seq"] // tm
    ctx_tiles = dims["ctx"] // tm
    kv_tiles = lat_tiles + ctx_tiles
    nb = dims["batch"]

    def mod_map(i):
        return (jnp.minimum(i // lat_tiles, nb), 0, 0)

    def kv_pos(i):
        lat_b = i // lat_tiles
        lat_pos = lat_b * kv_tiles + ctx_tiles + i % lat_tiles
        j = i - nb * lat_tiles
        ctx_pos = (j // ctx_tiles) * kv_tiles + j % ctx_tiles
        return jnp.where(i < nb * lat_tiles, lat_pos, ctx_pos)

    row = lambda w: pl.BlockSpec((tm, w), lambda i: (i, 0))
    kv_rows = nb * (dims["seq"] + dims["ctx"])
    return pl.pallas_call(
        _proj_in_kernel,
        grid=(nt,),
        in_specs=[
            row(D_MODEL),
            pl.BlockSpec((1, 6, D_MODEL), mod_map),
            row(LANE), row(LANE),
            _resident((D_MODEL, P_WIDTH)),
            _resident((1, GM_WIDTH)), _resident((1, GM_WIDTH)),
            _resident((GM_HEADS, GM_CHUNK, GM_CHUNK)), _resident((GM_CHUNK, GM_WIDTH)),
            _resident((1, Q_LORA)), _resident((Q_LORA, MLA_HEADS * HEAD_PAD)),
            _resident((1, KV_LORA)), _resident((KV_LORA, MLA_HEADS * QK_NOPE)),
            _resident((MLA_HEADS * V_DIM, KV_LORA)),
        ],
        out_specs=[
            row(GM_WIDTH), row(SSM_INNER), row(XBC_WIDTH), row(2 * LANE),
            row(MLA_HEADS * HEAD_PAD),
            pl.BlockSpec((tm, MLA_HEADS * HEAD_PAD), lambda i: (kv_pos(i), 0)),
            pl.BlockSpec((MLA_HEADS * V_DIM, tm), lambda i: (0, kv_pos(i))),
        ],
        out_shape=[
            jax.ShapeDtypeStruct((rows, GM_WIDTH), BF16),
            jax.ShapeDtypeStruct((rows, SSM_INNER), F32),
            jax.ShapeDtypeStruct((rows, XBC_WIDTH), F32),
            jax.ShapeDtypeStruct((rows, 2 * LANE), F32),
            jax.ShapeDtypeStruct((rows, MLA_HEADS * HEAD_PAD), BF16),
            jax.ShapeDtypeStruct((kv_rows, MLA_HEADS * HEAD_PAD), BF16),
            jax.ShapeDtypeStruct((MLA_HEADS * V_DIM, kv_rows), BF16),
        ],
        compiler_params=_cparams(("arbitrary",)),
        name="proj_in",
    )(x_all, modv, ca, cb, wp["w_in"], wp["g_v"], wp["b_v"], wp["w_sp"], wp["b_sp"],
      wp["g_q"], wp["w_q"], wp["g_kv"], wp["w_k"], wp["w_vt"])


def _ssd_kernel(backward, n_ctx, n_lat, *refs):
    if backward:
        (xp_ref, xc_ref, xn_ref, dt_ref, cw_ref, cbias_ref, dtb_ref, alog_ref, e_ref,
         yf_ref, z_ref, skip_ref, gssm_ref, out_ref, h_scr) = refs
    else:
        (xp_ref, xc_ref, xn_ref, dt_ref, cw_ref, cbias_ref, dtb_ref, alog_ref, e_ref,
         out_ref, h_scr) = refs
    q = SSM_CHUNK
    s = pl.program_id(1)
    is_ctx = s < n_ctx
    if backward:
        cc = jnp.where(is_ctx, n_ctx - 1 - s, n_lat - 1 - (s - n_ctx))
    else:
        cc = jnp.where(is_ctx, s, s - n_ctx)
    ncc = jnp.where(is_ctx, n_ctx, n_lat)
    at_start = cc == 0
    at_end = cc == ncc - 1

    @pl.when(s == 0)
    def _():
        h_scr[...] = jnp.zeros_like(h_scr)

    prev = jnp.where(at_start, 0.0, xp_ref[...])
    nxt = jnp.where(at_end, 0.0, xn_ref[...])
    ext = jnp.concatenate([prev, xc_ref[...], nxt], axis=0)
    acc = jnp.broadcast_to(cbias_ref[...], (q, XBC_WIDTH))
    base = 8 - SSM_CONV // 2
    for j in range(SSM_CONV):
        acc = acc + cw_ref[j:j + 1, :] * ext[base + j:base + j + q, :]
    u = _silu(acc)
    xc = u[:, :SSM_INNER]

    lane = lax.broadcasted_iota(jnp.int32, (q, LANE), 1)
    dtr = dt_ref[...] + dtb_ref[...]
    dtv = jnp.maximum(dtr, 0.0) + jnp.log1p(jnp.exp(-jnp.abs(dtr)))
    dtv = jnp.where(lane < SSM_HEADS, dtv, 0.0)
    da = dtv * (-jnp.exp(alog_ref[...]))

    ri = lax.broadcasted_iota(jnp.int32, (q, q), 0)
    ci = lax.broadcasted_iota(jnp.int32, (q, q), 1)
    mask = (ci >= ri) if backward else (ci <= ri)
    tri = jnp.where(mask, 1.0, 0.0).astype(F32)
    acs = jnp.dot(tri, da, precision=HIGHEST, preferred_element_type=F32)
    acs_t = acs.T
    last = 0 if backward else q - 1
    acs_last = acs[last:last + 1, :]
    small = jnp.concatenate([dtv, jnp.exp(acs_last - acs), jnp.exp(acs)], axis=0)
    wide = jnp.dot(small, e_ref[...], precision=HIGHEST, preferred_element_type=F32)
    dt_w = wide[:q]
    decay_w = wide[q:2 * q]
    eacs_w = wide[2 * q:]
    elast_w = eacs_w[last:last + 1, :]

    xdt = xc * dt_w
    xw_b = (xdt * decay_w).astype(BF16)
    h_old = h_scr[...]
    h_b = h_old.astype(BF16)

    lane_w = lax.broadcasted_iota(jnp.int32, (q, SSM_INNER), 1) // SSM_HEAD_DIM
    w_parts = []
    x_parts = []
    inter = []
    upd = []
    per_g = SSM_HEADS // SSM_GROUPS
    gw = per_g * SSM_HEAD_DIM
    for g in range(SSM_GROUPS):
        b_g = u[:, SSM_INNER + g * SSM_STATE:SSM_INNER + (g + 1) * SSM_STATE].astype(BF16)
        o_c = SSM_INNER + SSM_GROUPS * SSM_STATE
        c_g = u[:, o_c + g * SSM_STATE:o_c + (g + 1) * SSM_STATE].astype(BF16)
        cbm = lax.dot_general(c_g, b_g, (((1,), (1,)), ((), ())), preferred_element_type=F32)
        for hh in range(per_g):
            hd = g * per_g + hh
            seg = acs[:, hd:hd + 1] - acs_t[hd:hd + 1, :]
            lmat = jnp.exp(jnp.where(mask, seg, -jnp.inf))
            w_parts.append((cbm * lmat).astype(BF16))
            x_parts.append(jnp.where(lane_w == hd, xdt, 0.0).astype(BF16))
        inter.append(jnp.dot(c_g, h_b[:, g * gw:(g + 1) * gw], preferred_element_type=F32))
        upd.append(lax.dot_general(b_g, xw_b[:, g * gw:(g + 1) * gw], (((0,), (0,)), ((), ())),
                                   preferred_element_type=F32))
    y = jnp.dot(jnp.concatenate(w_parts, axis=1), jnp.concatenate(x_parts, axis=0),
                preferred_element_type=F32)
    y = y + jnp.concatenate(inter, axis=1) * eacs_w
    h_scr[...] = h_old * elast_w + jnp.concatenate(upd, axis=1)

    if backward:
        tot = yf_ref[...] + y + skip_ref[...] * xc
        yz = tot * _silu(z_ref[...])
        half = SSM_INNER // SSM_GROUPS
        outs = []
        for g in range(SSM_GROUPS):
            grp = yz[:, g * half:(g + 1) * half]
            outs.append(grp * lax.rsqrt(jnp.mean(grp * grp, axis=-1, keepdims=True) + RMS_EPS))
        out_ref[...] = (jnp.concatenate(outs, axis=1) * gssm_ref[...]).astype(BF16)
    else:
        out_ref[...] = y


def _ssd_sweep(backward, xbc, dtc, wp, dims, yf=None, z=None):
    nb = dims["batch"]
    n_lat = dims["seq"] // SSM_CHUNK
    n_ctx = dims["ctx"] // SSM_CHUNK
    ctx_base = nb * n_lat
    rows = xbc.shape[0]
    sub = SSM_CHUNK // 8

    def blk(b, s):
        is_ctx = s < n_ctx
        if backward:
            cc = jnp.where(is_ctx, n_ctx - 1 - s, n_lat - 1 - (s - n_ctx))
        else:
            cc = jnp.where(is_ctx, s, s - n_ctx)
        return jnp.where(is_ctx, ctx_base + b * n_ctx + cc, b * n_lat + cc)

    cur = lambda w: pl.BlockSpec((SSM_CHUNK, w), lambda b, s: (blk(b, s), 0))
    d = 1 if backward else 0
    in_specs = [
        pl.BlockSpec((8, XBC_WIDTH), lambda b, s: (jnp.maximum(blk(b, s) * sub - 1, 0), 0)),
        cur(XBC_WIDTH),
        pl.BlockSpec((8, XBC_WIDTH), lambda b, s: (jnp.minimum((blk(b, s) + 1) * sub, rows // 8 - 1), 0)),
        pl.BlockSpec((SSM_CHUNK, LANE), lambda b, s: (blk(b, s), d)),
        _resident((8, XBC_WIDTH)), _resident((1, XBC_WIDTH)),
        _resident((1, LANE)), _resident((1, LANE)), _resident((LANE, SSM_INNER)),
    ]
    args = [xbc, xbc, xbc, dtc, wp["conv_w"], wp["conv_b"], wp["dt_bias"][d], wp["a_log"][d], wp["expand"]]
    if backward:
        in_specs += [cur(SSM_INNER), cur(SSM_INNER), _resident((1, SSM_INNER)), _resident((1, SSM_INNER))]
        args += [yf, z, wp["d_skip"], wp["g_ssm"]]
    return pl.pallas_call(
        functools.partial(_ssd_kernel, backward, n_ctx, n_lat),
        grid=(nb, n_ctx + n_lat),
        in_specs=in_specs,
        out_specs=cur(SSM_INNER),
        out_shape=jax.ShapeDtypeStruct((rows, SSM_INNER), BF16 if backward else F32),
        scratch_shapes=[pltpu.VMEM((SSM_STATE, SSM_INNER), F32)],
        compiler_params=_cparams(("arbitrary", "arbitrary")),
        name="ssd_bwd" if backward else "ssd_fwd",
    )(*args)


def _attn_kernel(nk, tk, q_ref, k_ref, vt_ref, o_ref):
    q = q_ref[...]
    tq = q.shape[0]
    m = jnp.full((1, tq), -jnp.inf, F32)
    l = jnp.zeros((1, tq), F32)
    acc = jnp.zeros((V_DIM, tq), F32)
    for i in range(nk):
        k = k_ref[i * tk:(i + 1) * tk, :]
        vt = vt_ref[:, i * tk:(i + 1) * tk]
        st = lax.dot_general(k, q, (((1,), (1,)), ((), ())), preferred_element_type=F32)
        m_new = jnp.maximum(m, jnp.max(st, axis=0, keepdims=True))
        alpha = jnp.exp2(m - m_new)
        p = jnp.exp2(st - m_new)
        l = alpha * l + jnp.sum(p, axis=0, keepdims=True)
        acc = alpha * acc + jnp.dot(vt, p.astype(BF16), preferred_element_type=F32)
        m = m_new
    o_ref[...] = (acc / l).T.astype(BF16)


def _attention(q, k, vt, nb, lq, lk, q_row0, kv_stride, tq, tk):
    nq = lq // tq
    q0 = q_row0 // tq
    assert kv_stride % lk == 0 and lk % tk == 0 and q_row0 % tq == 0
    kvb = kv_stride // lk
    return pl.pallas_call(
        functools.partial(_attn_kernel, lk // tk, tk),
        grid=(nb, MLA_HEADS, nq),
        in_specs=[
            pl.BlockSpec((tq, HEAD_PAD), lambda b, h, i: (q0 + b * nq + i, h)),
            pl.BlockSpec((lk, HEAD_PAD), lambda b, h, i: (b * kvb, h)),
            pl.BlockSpec((V_DIM, lk), lambda b, h, i: (h, b * kvb)),
        ],
        out_specs=pl.BlockSpec((tq, V_DIM), lambda b, h, i: (b * nq + i, h)),
        out_shape=jax.ShapeDtypeStruct((nb * lq, MLA_HEADS * V_DIM), BF16),
        compiler_params=_cparams(("arbitrary", "arbitrary", "arbitrary")),
        name="mla_attention",
    )(q, k, vt)


def _out_proj_kernel(n_lat_tiles, gm_ref, ssm_ref, att_ref, attc_ref, x_ref, mod_ref, w_ref, g_ref, b_ref, wr_ref,
                     x1_ref, tok_ref, lg_ref):
    mod = mod_ref[0]
    att = att_ref[...]
    if attc_ref is not None:
        att = jnp.where(pl.program_id(0) >= n_lat_tiles, attc_ref[...], att)
    mixed = jnp.concatenate([gm_ref[...], ssm_ref[...], att], axis=1)
    mix = jnp.dot(mixed, w_ref[...], preferred_element_type=F32)
    x1 = _layer_norm(DN_ALPHA * x_ref[...] + mod[2:3, :] * mix, g_ref[...], b_ref[...])
    x1_ref[...] = x1
    tok = x1 * (1.0 + mod[4:5, :]) + mod[3:4, :]
    tok_ref[...] = _pack_rows(tok)
    lg_ref[...] = lax.dot_general(wr_ref[...], tok, (((1,), (1,)), ((), ())),
                                  precision=HIGHEST, preferred_element_type=F32)


def _out_proj_lat_kernel(n_lat_tiles, gm_ref, ssm_ref, att_ref, *rest):
    _out_proj_kernel(n_lat_tiles, gm_ref, ssm_ref, att_ref, None, *rest)


def _out_proj(gm, ssm, att, att_ctx, x_all, modv, wp, dims, rows):
    tm = ROW_TILE
    lat_tiles = dims["seq"] // tm
    nb = dims["batch"]
    n_lat = nb * lat_tiles
    row = lambda w: pl.BlockSpec((tm, w), lambda i: (i, 0))
    att_specs = [pl.BlockSpec((tm, MLA_HEADS * V_DIM), lambda i: (jnp.minimum(i, n_lat - 1), 0))]
    att_args = [att]
    body = _out_proj_lat_kernel
    if att_ctx is not None:
        att_specs.append(pl.BlockSpec((tm, MLA_HEADS * V_DIM), lambda i: (jnp.maximum(i - n_lat, 0), 0)))
        att_args.append(att_ctx)
        body = _out_proj_kernel
    return pl.pallas_call(
        functools.partial(body, n_lat),
        grid=(rows // tm,),
        in_specs=[
            row(GM_WIDTH), row(SSM_INNER), *att_specs, row(D_MODEL),
            pl.BlockSpec((1, 6, D_MODEL), lambda i: (jnp.minimum(i // lat_tiles, nb), 0, 0)),
            _resident((D_MODEL, D_MODEL)), _resident((1, D_MODEL)), _resident((1, D_MODEL)),
            _resident((N_EXPERTS, D_MODEL)),
        ],
        out_specs=[row(D_MODEL), row(D_MODEL // 2), pl.BlockSpec((N_EXPERTS, tm), lambda i: (0, i))],
        out_shape=[
            jax.ShapeDtypeStruct((rows, D_MODEL), F32),
            jax.ShapeDtypeStruct((rows, D_MODEL // 2), jnp.uint32),
            jax.ShapeDtypeStruct((N_EXPERTS, rows), F32),
        ],
        compiler_params=_cparams(("arbitrary",)),
        name="out_proj",
    )(gm, ssm, *att_args, x_all, modv, wp["w_out"], wp["ln1_g"], wp["ln1_b"], wp["w_router_t"])


def _gate_kernel(lg_ref, b_ref, eidx_ref, w_ref, pos_ref, cnt_ref, carry_scr):
    per = N_EXPERTS // N_EXPERT_GROUPS
    tn = lg_ref.shape[1]
    sc = jax.nn.sigmoid(lg_ref[...])
    bi = sc + b_ref[...]
    neg = np.float32(-np.inf)
    xs = [bi[j * N_EXPERT_GROUPS:(j + 1) * N_EXPERT_GROUPS, :] for j in range(per)]
    ss = [sc[j * N_EXPERT_GROUPS:(j + 1) * N_EXPERT_GROUPS, :] for j in range(per)]
    m1 = functools.reduce(jnp.maximum, xs)
    first = functools.reduce(jnp.minimum, [jnp.where(xs[j] == m1, j, per) for j in range(per)])
    m2 = functools.reduce(jnp.maximum, [jnp.where(first == j, neg, xs[j]) for j in range(per)])
    gscore = m1 + m2
    gi = lax.broadcasted_iota(jnp.int32, (N_EXPERT_GROUPS, tn), 0)
    cur = gscore
    gsel = jnp.zeros((N_EXPERT_GROUPS, tn), jnp.int32)
    for _ in range(TOPK_GROUPS):
        mx = jnp.max(cur, axis=0, keepdims=True)
        fi = jnp.min(jnp.where(cur == mx, gi, N_EXPERT_GROUPS), axis=0, keepdims=True)
        pick = gi == fi
        gsel = jnp.where(pick, 1, gsel)
        cur = jnp.where(pick, neg, cur)
    cur = [jnp.where(gsel > 0, xs[j], neg) for j in range(per)]
    eid = [gi * per + j for j in range(per)]
    sel = [jnp.zeros((N_EXPERT_GROUPS, tn), F32) for _ in range(per)]
    picked = []
    for _ in range(TOP_K):
        mx = jnp.max(functools.reduce(jnp.maximum, cur), axis=0, keepdims=True)
        cand = functools.reduce(jnp.minimum, [jnp.where(cur[j] == mx, eid[j], N_EXPERTS) for j in range(per)])
        fi = jnp.min(cand, axis=0, keepdims=True)
        wk = jnp.zeros((N_EXPERT_GROUPS, tn), F32)
        for j in range(per):
            pick = eid[j] == fi
            sel[j] = jnp.where(pick, 1.0, sel[j])
            cur[j] = jnp.where(pick, neg, cur[j])
            wk = wk + jnp.where(pick, ss[j], 0.0)
        picked.append((fi, jnp.sum(wk, axis=0, keepdims=True)))
    tot = functools.reduce(jnp.add, [wk for _, wk in picked])

    @pl.when(pl.program_id(0) == 0)
    def _():
        carry_scr[...] = jnp.zeros_like(carry_scr)

    sel_all = jnp.concatenate(sel, axis=0)
    ri = lax.broadcasted_iota(jnp.int32, (tn, tn), 0)
    ci = lax.broadcasted_iota(jnp.int32, (tn, tn), 1)
    before = jnp.where(ri < ci, 1.0, 0.0).astype(BF16)
    prefix = jnp.dot(sel_all.astype(BF16), before, preferred_element_type=F32)
    carry = carry_scr[...]
    rank = prefix + carry
    carry_new = carry + prefix[:, tn - 1:tn] + sel_all[:, tn - 1:tn]
    carry_scr[...] = carry_new
    cnt_ref[...] = jnp.broadcast_to(carry_new, cnt_ref.shape).astype(jnp.int32)
    for k, (fi, wk) in enumerate(picked):
        pk = jnp.zeros((N_EXPERT_GROUPS, tn), F32)
        for j in range(per):
            pk = pk + jnp.where(eid[j] == fi, rank[j * N_EXPERT_GROUPS:(j + 1) * N_EXPERT_GROUPS, :], 0.0)
        eidx_ref[k:k + 1, :] = fi
        w_ref[k:k + 1, :] = wk / tot * ROUTED_SCALE
        pos_ref[k:k + 1, :] = jnp.sum(pk, axis=0, keepdims=True).astype(jnp.int32)


def _gate(logits_t, b_router_col):
    t = logits_t.shape[1]
    tn = 512
    tile = pl.BlockSpec((TOP_K, tn), lambda i: (0, i))
    return pl.pallas_call(
        _gate_kernel,
        grid=(t // tn,),
        in_specs=[pl.BlockSpec((N_EXPERTS, tn), lambda i: (0, i)), _resident((N_EXPERTS, 1))],
        out_specs=[tile, tile, tile, pl.BlockSpec((N_EXPERTS, LANE), lambda i: (0, 0))],
        out_shape=[
            jax.ShapeDtypeStruct((TOP_K, t), jnp.int32),
            jax.ShapeDtypeStruct((TOP_K, t), F32),
            jax.ShapeDtypeStruct((TOP_K, t), jnp.int32),
            jax.ShapeDtypeStruct((N_EXPERTS, LANE), jnp.int32),
        ],
        scratch_shapes=[pltpu.VMEM((N_EXPERTS, 1), F32)],
        compiler_params=_cparams(("arbitrary",)),
        name="gate_topk",
    )(logits_t, b_router_col)


DISPATCH_TILE = 512


def _dispatch_kernel(padlo_ref, padn_ref, nu_ref, slot_ref, tok_ref, xs_hbm, zero_buf, sem, zsem):
    i = pl.program_id(0)
    td = slot_ref.shape[1]

    def row_copy(r, k):
        return pltpu.make_async_copy(tok_ref.at[pl.ds(r, 1)], xs_hbm.at[pl.ds(slot_ref[k, r], 1)], sem)

    def issue(r, carry):
        for k in range(TOP_K):
            row_copy(r, k).start()
        return carry

    lax.fori_loop(0, td, issue, 0)

    @pl.when(i == 0)
    def _():
        zero_buf[...] = jnp.zeros_like(zero_buf)

        def fill(e, carry):
            lo = padlo_ref[e]
            n = padn_ref[e]
            head = jnp.minimum((-lo) & 7, n)

            for s in range(7):
                @pl.when(s < head)
                def _(s=s):
                    cp = pltpu.make_async_copy(zero_buf.at[pl.ds(0, 1)], xs_hbm.at[pl.ds(lo + s, 1)], zsem)
                    cp.start()
                    cp.wait()

            rest = n - head
            off = lo + head
            size = EXPERT_BLOCK // 2
            while size >= 8:
                hit = (rest & size) != 0

                @pl.when(hit)
                def _(off=off, size=size):
                    dst = xs_hbm.at[pl.ds(pl.multiple_of(off, 8), size)]
                    cp = pltpu.make_async_copy(zero_buf.at[pl.ds(0, size)], dst, zsem)
                    cp.start()
                    cp.wait()

                off = off + jnp.where(hit, size, 0)
                size //= 2
            return carry

        lax.fori_loop(0, N_EXPERTS, fill, 0)

        zrows = zero_buf.shape[0]

        def fill_tail(b, carry):
            for h in range(EXPERT_BLOCK // zrows):
                row0 = pl.multiple_of(b * EXPERT_BLOCK + h * zrows, 8)
                cp = pltpu.make_async_copy(zero_buf, xs_hbm.at[pl.ds(row0, zrows)], zsem)
                cp.start()
                cp.wait()
            return carry

        lax.fori_loop(nu_ref[0], xs_hbm.shape[0] // EXPERT_BLOCK, fill_tail, 0)

    for _ in range(TOP_K):
        pltpu.make_async_copy(tok_ref, xs_hbm.at[pl.ds(0, td)], sem).wait()


def _dispatch_rows(slot, pad_lo, pad_n, n_used, tok, cap):
    t = tok.shape[0]
    td = DISPATCH_TILE
    return pl.pallas_call(
        _dispatch_kernel,
        grid_spec=pltpu.PrefetchScalarGridSpec(
            num_scalar_prefetch=3,
            grid=(t // td,),
            in_specs=[
                pl.BlockSpec((TOP_K, td), lambda i, lo, n, nu: (0, i), memory_space=pltpu.SMEM),
                pl.BlockSpec((td, D_MODEL // 2), lambda i, lo, n, nu: (i, 0)),
            ],
            out_specs=pl.BlockSpec(memory_space=pl.ANY),
            scratch_shapes=[pltpu.VMEM((EXPERT_BLOCK // 2, D_MODEL // 2), jnp.uint32),
                            pltpu.SemaphoreType.DMA, pltpu.SemaphoreType.DMA],
        ),
        out_shape=jax.ShapeDtypeStruct((cap, D_MODEL // 2), jnp.uint32),
        compiler_params=_cparams(("arbitrary",)),
        name="dispatch_rows",
    )(pad_lo, pad_n, n_used, slot, tok)


def _experts_kernel(be_ref, nu_ref, x_ref, wg_ref, wu_ref, wd_ref, y_ref, wg_s, wu_s, wd_s):
    j = pl.program_id(0)
    prev = be_ref[jnp.maximum(j - 1, 0)]
    fresh = jnp.logical_or(j == 0, be_ref[j] != prev)

    @pl.when(fresh)
    def _():
        wg_s[...] = wg_ref[0].astype(BF16)
        wu_s[...] = wu_ref[0].astype(BF16)
        wd_s[...] = wd_ref[0].astype(BF16)

    @pl.when(j < nu_ref[0])
    def _():
        x = _unpack_rows(x_ref[...])
        g = jnp.dot(x, wg_s[...], preferred_element_type=F32)
        u = jnp.dot(x, wu_s[...], preferred_element_type=F32)
        hb = (_silu(g) * u).astype(BF16)
        y_ref[...] = jnp.dot(hb, wd_s[...], preferred_element_type=F32)

    @pl.when(j >= nu_ref[0])
    def _():
        y_ref[...] = jnp.zeros_like(y_ref)


def _experts(layer, block_e, n_used, x_sorted, w_gate, w_up, w_down):
    cap = x_sorted.shape[0]
    blk = EXPERT_BLOCK
    return pl.pallas_call(
        _experts_kernel,
        grid_spec=pltpu.PrefetchScalarGridSpec(
            num_scalar_prefetch=2,
            grid=(cap // blk,),
            in_specs=[
                pl.BlockSpec((blk, D_MODEL // 2), lambda j, be, nu: (jnp.minimum(j, nu[0] - 1), 0)),
                pl.BlockSpec((None, 1, D_MODEL, EXPERT_FF), lambda j, be, nu: (layer, be[j], 0, 0)),
                pl.BlockSpec((None, 1, D_MODEL, EXPERT_FF), lambda j, be, nu: (layer, be[j], 0, 0)),
                pl.BlockSpec((None, 1, EXPERT_FF, D_MODEL), lambda j, be, nu: (layer, be[j], 0, 0)),
            ],
            out_specs=pl.BlockSpec((blk, D_MODEL), lambda j, be, nu: (j, 0)),
            scratch_shapes=[pltpu.VMEM((D_MODEL, EXPERT_FF), BF16), pltpu.VMEM((D_MODEL, EXPERT_FF), BF16),
                            pltpu.VMEM((EXPERT_FF, D_MODEL), BF16)],
        ),
        out_shape=jax.ShapeDtypeStruct((cap, D_MODEL), F32),
        compiler_params=_cparams(("arbitrary",)),
        name="routed_experts",
    )(block_e, n_used, x_sorted, w_gate, w_up, w_down)


COMBINE_TILE = 128


def _final_kernel(slot_ref, slot_next_ref, w8_ref, tok_ref, x1_ref, mod_ref, wgu_ref, wd_ref, g_ref, b_ref,
                  y_hbm, o_ref, ybuf, sems):
    i = pl.program_id(0)
    n = pl.num_programs(0)
    tf = tok_ref.shape[0]
    cur = i % 2

    def gather(slots, buf):
        def issue(r, carry):
            for k in range(TOP_K):
                pltpu.make_async_copy(y_hbm.at[pl.ds(slots[k, r], 1)], ybuf.at[buf, k, pl.ds(r, 1)],
                                      sems.at[buf]).start()
            return carry

        lax.fori_loop(0, tf, issue, 0)

    @pl.when(i == 0)
    def _():
        gather(slot_ref, 0)

    @pl.when(i + 1 < n)
    def _():
        gather(slot_next_ref, 1 - cur)

    mod = mod_ref[0]
    gu = jnp.dot(_unpack_rows(tok_ref[...]), wgu_ref[...], preferred_element_type=F32)
    hb = (_silu(gu[:, :SHARED_FF]) * gu[:, SHARED_FF:]).astype(BF16)
    f = jnp.dot(hb, wd_ref[...], preferred_element_type=F32)

    for k in range(TOP_K):
        pltpu.make_async_copy(y_hbm.at[pl.ds(0, tf)], ybuf.at[cur, k], sems.at[cur]).wait()
    w8 = w8_ref[...]
    for k in range(TOP_K):
        f = f + w8[:, k:k + 1] * ybuf[cur, k]
    o_ref[...] = _layer_norm(DN_ALPHA * x1_ref[...] + mod[5:6, :] * f, g_ref[...], b_ref[...])


def _final(slot, w8_t, tok, y_sorted, x1, modv, wp, dims, rows):
    tf = COMBINE_TILE
    nt = rows // tf
    lat_tiles = dims["seq"] // tf
    nb = dims["batch"]
    row = lambda w: pl.BlockSpec((tf, w), lambda i: (i, 0))
    return pl.pallas_call(
        _final_kernel,
        grid=(nt,),
        in_specs=[
            pl.BlockSpec((TOP_K, tf), lambda i: (0, i), memory_space=pltpu.SMEM),
            pl.BlockSpec((TOP_K, tf), lambda i: (0, jnp.minimum(i + 1, nt - 1)), memory_space=pltpu.SMEM),
            row(TOP_K), row(D_MODEL // 2), row(D_MODEL),
            pl.BlockSpec((1, 6, D_MODEL), lambda i: (jnp.minimum(i // lat_tiles, nb), 0, 0)),
            _resident((D_MODEL, 2 * SHARED_FF)), _resident((SHARED_FF, D_MODEL)),
            _resident((1, D_MODEL)), _resident((1, D_MODEL)),
            pl.BlockSpec(memory_space=pl.ANY),
        ],
        out_specs=row(D_MODEL),
        out_shape=jax.ShapeDtypeStruct((rows, D_MODEL), F32),
        scratch_shapes=[pltpu.VMEM((2, TOP_K, tf, D_MODEL), F32), pltpu.SemaphoreType.DMA((2,))],
        compiler_params=_cparams(("arbitrary",)),
        name="combine_shared_ln2",
    )(slot, slot, w8_t, tok, x1, modv, wp["w_sh_gu"], wp["w_sh_down"], wp["ln2_g"], wp["ln2_b"], y_sorted)


def _rot_half_cols():
    src = np.zeros((QK_ROPE,), np.int32)
    sign = np.zeros((QK_ROPE,), np.float32)
    quarter = QK_ROPE // 4
    for dcol in range(QK_ROPE):
        part, i = divmod(dcol, 2 * quarter)
        half, kk = divmod(i, quarter)
        src[dcol] = part * 2 * quarter + (quarter + kk if half == 0 else kk)
        sign[dcol] = -1.0 if half == 0 else 1.0
    return src, sign


def _expert_major(a, axis):
    per = N_EXPERTS // N_EXPERT_GROUPS
    shp = a.shape
    a = a.reshape(shp[:axis] + (N_EXPERT_GROUPS, per) + shp[axis + 1:])
    a = jnp.swapaxes(a, axis, axis + 1)
    return a.reshape(shp)


def _prep_layer(l, p):
    src, sign = _rot_half_cols()
    w_in = p["w_in"][l]
    col = lambda lo, n: w_in[:, lo:lo + n]
    zpad = jnp.zeros((D_MODEL, LANE - SSM_HEADS), F32)
    w_kr = col(R_KR, QK_ROPE)
    w_in_p = jnp.concatenate([
        col(R_GM, 2 * GM_WIDTH), col(R_Q, Q_LORA), col(R_Z, SSM_INNER), col(R_XBC, XBC_WIDTH),
        col(R_DT, SSM_HEADS), zpad, col(R_DT + SSM_HEADS, SSM_HEADS), zpad,
        col(R_KV, KV_LORA), w_kr, w_kr[:, src] * sign,
    ], axis=1).astype(BF16)
    wq = p["w_q_b"][l].reshape(Q_LORA, MLA_HEADS, QK_NOPE + QK_ROPE)
    wq_rope = wq[:, :, QK_NOPE:]
    w_q = jnp.concatenate([wq[:, :, :QK_NOPE], wq_rope, wq_rope[:, :, src] * sign], axis=2)
    w_q = w_q.reshape(Q_LORA, MLA_HEADS * HEAD_PAD).astype(BF16)
    wkv = p["w_kv_b"][l].reshape(KV_LORA, MLA_HEADS, QK_NOPE + V_DIM)
    w_k = wkv[:, :, :QK_NOPE].reshape(KV_LORA, -1).astype(BF16)
    w_vt = wkv[:, :, QK_NOPE:].reshape(KV_LORA, -1).T.astype(BF16)
    lane_pad = lambda a: jnp.pad(a, ((0, 0), (0, LANE - a.shape[1])))
    expand = jnp.repeat(jnp.eye(LANE, SSM_HEADS, dtype=F32), SSM_HEAD_DIM, axis=1)
    r2 = lambda a: a.reshape(1, -1)
    return dict(
        w_in=w_in_p, g_v=r2(p["g_v"][l]), b_v=r2(p["b_v"][l]), w_sp=p["w_sp"][l].astype(BF16),
        b_sp=jnp.repeat(p["b_sp"][l].T, GM_CHUNK, axis=1),
        g_q=r2(p["g_q"][l]), w_q=w_q, g_kv=r2(p["g_kv"][l]), w_k=w_k, w_vt=w_vt,
        conv_w=jnp.pad(p["conv_w"][l], ((0, 8 - SSM_CONV), (0, 0))), conv_b=r2(p["conv_b"][l]),
        dt_bias=lane_pad(p["dt_bias"][l])[:, None, :], a_log=lane_pad(p["a_log"][l])[:, None, :],
        expand=expand, d_skip=r2(jnp.repeat(p["d_skip"][l], SSM_HEAD_DIM)), g_ssm=r2(p["g_ssm"][l]),
        w_out=p["w_out"][l].astype(BF16), ln1_g=r2(p["ln1_g"][l]), ln1_b=r2(p["ln1_b"][l]),
        ln2_g=r2(p["ln2_g"][l]), ln2_b=r2(p["ln2_b"][l]),
        w_router_t=_expert_major(p["w_router"][l].T, 0),
        b_router=_expert_major(p["b_router"][l], 0).reshape(N_EXPERTS, 1),
        w_sh_gu=jnp.concatenate([p["w_sh_gate"][l], p["w_sh_up"][l]], axis=1).astype(BF16),
        w_sh_down=p["w_sh_down"][l].astype(BF16),
    )


def _rope_tables(nb, seq, nctx):
    t = jnp.arange(seq)
    half = QK_ROPE // 2
    inv = 1.0 / (ROPE_BASE ** (jnp.arange(0, half, 2, dtype=F32) / half))
    ang_r = (t // GRID_W)[:, None] * inv
    ang_c = (t % GRID_W)[:, None] * inv
    ang = jnp.concatenate([ang_r, ang_r, ang_c, ang_c], axis=1)
    zero = jnp.zeros((seq, LANE - QK_ROPE), F32)
    ca_lat = jnp.concatenate([jnp.cos(ang), zero], axis=1)
    cb_lat = jnp.concatenate([jnp.sin(ang), zero], axis=1)
    ca_ctx = jnp.concatenate([jnp.ones((nb * nctx, QK_ROPE), F32), jnp.zeros((nb * nctx, LANE - QK_ROPE), F32)], axis=1)
    ca = jnp.concatenate([jnp.tile(ca_lat, (nb, 1)), ca_ctx], axis=0)
    cb = jnp.concatenate([jnp.tile(cb_lat, (nb, 1)), jnp.zeros((nb * nctx, LANE), F32)], axis=0)
    return ca, cb


def _slot_layout(eidx, pos, cnt, t):
    per = N_EXPERTS // N_EXPERT_GROUPS
    blk = EXPERT_BLOCK
    counts = cnt[:, 0].reshape(per, N_EXPERT_GROUPS).T.reshape(N_EXPERTS)
    padded = (counts + blk - 1) // blk * blk
    pend = jnp.cumsum(padded)
    pstart = pend - padded
    n_blocks = (t * TOP_K + N_EXPERTS * (blk - 1) + blk - 1) // blk
    block_e = jnp.sum((pend[None, :] <= (jnp.arange(n_blocks) * blk)[:, None]).astype(jnp.int32), axis=1)
    block_e = jnp.minimum(block_e, N_EXPERTS - 1)
    n_used = (pend[-1] // blk).reshape(1)
    hit = eidx[:, :, None] == jnp.arange(N_EXPERTS, dtype=jnp.int32)
    slot = (jnp.sum(jnp.where(hit, pstart, 0), axis=2) + pos).astype(jnp.int32)
    return (slot, block_e.astype(jnp.int32), n_used.astype(jnp.int32), (pstart + counts).astype(jnp.int32),
            (padded - counts).astype(jnp.int32), n_blocks * blk)


def kernel(x, c, ctx, c_ctx, w_mod, b_mod, w_in, g_q, w_q_b, g_kv, w_kv_b, conv_w, conv_b, a_log, dt_bias, d_skip, g_ssm, g_v, b_v, w_sp, b_sp, w_out, ln1_g, ln1_b, ln2_g, ln2_b, w_router, b_router, w_e_gate, w_e_up, w_e_down, w_sh_gate, w_sh_up, w_sh_down):
    p = dict(w_in=w_in, g_q=g_q, w_q_b=w_q_b, g_kv=g_kv, w_kv_b=w_kv_b, conv_w=conv_w, conv_b=conv_b,
             a_log=a_log, dt_bias=dt_bias, d_skip=d_skip, g_ssm=g_ssm, g_v=g_v, b_v=b_v, w_sp=w_sp, b_sp=b_sp,
             w_out=w_out, ln1_g=ln1_g, ln1_b=ln1_b, ln2_g=ln2_g, ln2_b=ln2_b, w_router=w_router,
             b_router=b_router, w_sh_gate=w_sh_gate, w_sh_up=w_sh_up, w_sh_down=w_sh_down)
    nb, seq, d = x.shape
    nctx = ctx.shape[1]
    depth = w_mod.shape[0]
    assert d == D_MODEL and depth == DEPTH and nb + 1 <= 8
    assert seq % ROW_TILE == 0 and nctx % ROW_TILE == 0 and seq % GRID_W == 0
    dims = dict(batch=nb, seq=seq, ctx=nctx)
    lat_rows = nb * seq
    rows_all = lat_rows + nb * nctx
    lk = seq + nctx

    x_all = jnp.concatenate([x.reshape(lat_rows, d), ctx.reshape(nb * nctx, d)], axis=0)
    cvec = jnp.zeros((8, d), F32).at[:nb].set(c).at[nb].set(c_ctx)
    mods = _modulation(cvec, w_mod, b_mod).reshape(depth, 8, 6, d)[:, :nb + 1]
    ca, cb = _rope_tables(nb, seq, nctx)
    tq = min(1024, seq)
    tk = 768 if lk % 768 == 0 else ROW_TILE

    for l in range(depth):
        last = l == depth - 1
        wp = _prep_layer(l, p)
        modv = mods[l]
        gm, z, xbc, dtc, q, k, vt = _proj_in(x_all, modv, ca, cb, wp, dims)
        yf = _ssd_sweep(False, xbc, dtc, wp, dims)
        ssm = _ssd_sweep(True, xbc, dtc, wp, dims, yf=yf, z=z)
        att = _attention(q, k, vt, nb, seq, lk, 0, lk, tq, tk)
        rows = lat_rows if last else rows_all
        att_ctx = None if last else _attention(q, k, vt, nb, nctx, nctx, lat_rows, lk, ROW_TILE, ROW_TILE)
        x1, tok, logits_t = _out_proj(gm, ssm, att, att_ctx, x_all, modv, wp, dims, rows)
        eidx, w8, pos, cnt = _gate(logits_t, wp["b_router"])
        slot, block_e, n_used, pad_lo, pad_n, cap = _slot_layout(eidx, pos, cnt, rows)
        x_sorted = _dispatch_rows(slot, pad_lo, pad_n, n_used, tok, cap)
        y_sorted = _experts(l, block_e, n_used, x_sorted, w_e_gate, w_e_up, w_e_down)
        x_all = _final(slot, w8.T, tok, y_sorted, x1, modv, wp, dims, rows)
    return x_all.reshape(nb, seq, d)
```

```python
import functools
import math

import jax
import jax.numpy as jnp
import numpy as np
from jax import lax
from jax.experimental import pallas as pl
from jax.experimental.pallas import tpu as pltpu

F32 = jnp.float32
BF16 = jnp.bfloat16
HIGHEST = lax.Precision.HIGHEST

D_MODEL = 2048
GRID_W = 64
GM_HEADS = 4
GM_WIDTH = 512
GM_CHUNK = 128
SSM_HEADS = 8
SSM_HEAD_DIM = 64
SSM_INNER = 512
SSM_GROUPS = 2
SSM_STATE = 128
SSM_CONV = 5
SSM_CHUNK = 128
XBC_WIDTH = 1024
MLA_HEADS = 8
QK_NOPE = 128
QK_ROPE = 64
V_DIM = 128
Q_LORA = 768
KV_LORA = 256
ROPE_BASE = 10000.0
N_EXPERTS = 64
TOP_K = 8
N_EXPERT_GROUPS = 8
TOPK_GROUPS = 4
EXPERT_FF = 512
SHARED_FF = 512
ROUTED_SCALE = 2.5
LN_EPS = 1e-5
RMS_EPS = 1e-6
DEPTH = 2
DN_ALPHA = (2 * DEPTH) ** 0.25

R_GM = 0
R_Q = R_GM + 2 * GM_WIDTH
R_Z = R_Q + Q_LORA
R_XBC = R_Z + SSM_INNER
R_DT = R_XBC + XBC_WIDTH
R_KV = R_DT + 2 * SSM_HEADS
R_KR = R_KV + KV_LORA

LANE = 128
P_GM = 0
P_Q = P_GM + 2 * GM_WIDTH
P_Z = P_Q + Q_LORA
P_XBC = P_Z + SSM_INNER
P_DT = P_XBC + XBC_WIDTH
P_KV = P_DT + 2 * LANE
P_KR = P_KV + KV_LORA
P_WIDTH = P_KR + LANE

HEAD_PAD = 256
ROW_TILE = 256
OUT_TILE = 2 * ROW_TILE
EXPERT_BLOCK = 256
VMEM_LIMIT = 56 * 1024 * 1024


def _cparams(sem):
    return pltpu.CompilerParams(dimension_semantics=sem, vmem_limit_bytes=VMEM_LIMIT)


def _resident(shape):
    n = len(shape)
    return pl.BlockSpec(shape, lambda *_: (0,) * n, pipeline_mode=pl.Buffered(1))


def _silu(x):
    return x * jax.nn.sigmoid(x)


def _layer_norm(y, g, b):
    mu = jnp.mean(y, axis=-1, keepdims=True)
    yc = y - mu
    var = jnp.mean(yc * yc, axis=-1, keepdims=True)
    return yc * lax.rsqrt(var + LN_EPS) * g + b


def _rms_norm(y, g):
    return y * lax.rsqrt(jnp.mean(y * y, axis=-1, keepdims=True) + RMS_EPS) * g


def _pack_rows(t):
    n = t.shape[1] // 2
    bits = pltpu.bitcast(t.astype(BF16).astype(F32), jnp.uint32)
    return (bits[:, :n] >> 16) | bits[:, n:]


def _unpack_rows(w):
    lo = pltpu.bitcast(w << 16, F32).astype(BF16)
    hi = pltpu.bitcast(w & jnp.uint32(0xFFFF0000), F32).astype(BF16)
    return jnp.concatenate([lo, hi], axis=1)


def _mod_kernel(c_ref, w_ref, b_ref, o_ref):
    c = c_ref[...]
    o_ref[0] = jnp.dot(_silu(c), w_ref[0], precision=HIGHEST, preferred_element_type=F32) + b_ref[0]


def _modulation(cvec, w_mod, b_mod):
    depth, d, n = w_mod.shape
    tn = 1536
    return pl.pallas_call(
        _mod_kernel,
        grid=(depth, n // tn),
        in_specs=[
            pl.BlockSpec((8, d), lambda l, j: (0, 0)),
            pl.BlockSpec((1, d, tn), lambda l, j: (l, 0, j)),
            pl.BlockSpec((1, 1, tn), lambda l, j: (l, 0, j)),
        ],
        out_specs=pl.BlockSpec((1, 8, tn), lambda l, j: (l, 0, j)),
        out_shape=jax.ShapeDtypeStruct((depth, 8, n), F32),
        compiler_params=_cparams(("arbitrary", "arbitrary")),
        name="modulation",
    )(cvec, w_mod, b_mod.reshape(depth, 1, n))


def _proj_in_kernel(x_ref, mod_ref, ca_ref, cb_ref, w_ref, gv_ref, bv_ref, wsp_ref, bsp_ref,
                    gq_ref, wq_ref, gkv_ref, wk_ref, wvt_ref,
                    gm_ref, z_ref, xbc_ref, dt_ref, q_ref, k_ref, vt_ref):
    tm = x_ref.shape[0]
    mod = mod_ref[0]
    h = (x_ref[...] * (1.0 + mod[1:2, :]) + mod[0:1, :]).astype(BF16)

    def proj(lo, n):
        return jnp.dot(h, w_ref[:, lo:lo + n], preferred_element_type=F32)

    uv = proj(P_GM, 2 * GM_WIDTH)
    gl = 0.5 * uv * (1.0 + lax.erf(uv * np.float32(math.sqrt(0.5))))
    u = gl[:, :GM_WIDTH]
    vn = _layer_norm(gl[:, GM_WIDTH:], gv_ref[...], bv_ref[...]).astype(BF16)
    for c in range(tm // GM_CHUNK):
        r0 = c * GM_CHUNK
        parts = [jnp.dot(wsp_ref[hh], vn[r0:r0 + GM_CHUNK, hh * LANE:(hh + 1) * LANE],
                         preferred_element_type=F32) for hh in range(GM_HEADS)]
        s = jnp.concatenate(parts, axis=1) + bsp_ref[...]
        gm_ref[r0:r0 + GM_CHUNK, :] = (u[r0:r0 + GM_CHUNK, :] * s).astype(BF16)

    z_ref[...] = proj(P_Z, SSM_INNER)
    xbc_ref[...] = proj(P_XBC, XBC_WIDTH)
    dt_ref[...] = proj(P_DT, 2 * LANE)

    ca = ca_ref[...]
    cb = cb_ref[...]
    scale = np.float32((QK_NOPE + QK_ROPE) ** -0.5 * math.log2(math.e))

    def rope(t):
        return t * ca + pltpu.roll(t, QK_ROPE, 1) * cb

    cqn = _rms_norm(proj(P_Q, Q_LORA), gq_ref[...]).astype(BF16)
    yq = jnp.dot(cqn, wq_ref[...], preferred_element_type=F32)
    for hh in range(MLA_HEADS):
        c0 = hh * HEAD_PAD
        q_ref[:, c0:c0 + LANE] = (yq[:, c0:c0 + LANE] * scale).astype(BF16)
        q_ref[:, c0 + LANE:c0 + HEAD_PAD] = (rope(yq[:, c0 + LANE:c0 + HEAD_PAD]) * scale).astype(BF16)

    ckvn = _rms_norm(proj(P_KV, KV_LORA), gkv_ref[...]).astype(BF16)
    krf = rope(proj(P_KR, LANE)).astype(BF16)
    yk = jnp.dot(ckvn, wk_ref[...], preferred_element_type=F32)
    for hh in range(MLA_HEADS):
        c0 = hh * HEAD_PAD
        k_ref[:, c0:c0 + LANE] = yk[:, hh * LANE:(hh + 1) * LANE].astype(BF16)
        k_ref[:, c0 + LANE:c0 + HEAD_PAD] = krf
    vt_ref[...] = lax.dot_general(wvt_ref[...], ckvn, (((1,), (1,)), ((), ())),
                                  preferred_element_type=F32).astype(BF16)


def _proj_in(x_all, modv, ca, cb, wp, dims):
    rows = x_all.shape[0]
    tm = ROW_TILE
    nt = rows // tm
    lat_tiles = dims["seq"] // tm
    ctx_tiles = dims["ctx"] // tm
    kv_tiles = lat_tiles + ctx_tiles
    nb = dims["batch"]

    def mod_map(i):
        return (jnp.minimum(i // lat_tiles, nb), 0, 0)

    def kv_pos(i):
        lat_b = i // lat_tiles
        lat_pos = lat_b * kv_tiles + ctx_tiles + i % lat_tiles
        j = i - nb * lat_tiles
        ctx_pos = (j // ctx_tiles) * kv_tiles + j % ctx_tiles
        return jnp.where(i < nb * lat_tiles, lat_pos, ctx_pos)

    row = lambda w: pl.BlockSpec((tm, w), lambda i: (i, 0))
    kv_rows = nb * (dims["seq"] + dims["ctx"])
    return pl.pallas_call(
        _proj_in_kernel,
        grid=(nt,),
        in_specs=[
            row(D_MODEL),
            pl.BlockSpec((1, 6, D_MODEL), mod_map),
            row(LANE), row(LANE),
            _resident((D_MODEL, P_WIDTH)),
            _resident((1, GM_WIDTH)), _resident((1, GM_WIDTH)),
            _resident((GM_HEADS, GM_CHUNK, GM_CHUNK)), _resident((GM_CHUNK, GM_WIDTH)),
            _resident((1, Q_LORA)), _resident((Q_LORA, MLA_HEADS * HEAD_PAD)),
            _resident((1, KV_LORA)), _resident((KV_LORA, MLA_HEADS * QK_NOPE)),
            _resident((MLA_HEADS * V_DIM, KV_LORA)),
        ],
        out_specs=[
            row(GM_WIDTH), row(SSM_INNER), row(XBC_WIDTH), row(2 * LANE),
            row(MLA_HEADS * HEAD_PAD),
            pl.BlockSpec((tm, MLA_HEADS * HEAD_PAD), lambda i: (kv_pos(i), 0)),
            pl.BlockSpec((MLA_HEADS * V_DIM, tm), lambda i: (0, kv_pos(i))),
        ],
        out_shape=[
            jax.ShapeDtypeStruct((rows, GM_WIDTH), BF16),
            jax.ShapeDtypeStruct((rows, SSM_INNER), F32),
            jax.ShapeDtypeStruct((rows, XBC_WIDTH), F32),
            jax.ShapeDtypeStruct((rows, 2 * LANE), F32),
            jax.ShapeDtypeStruct((rows, MLA_HEADS * HEAD_PAD), BF16),
            jax.ShapeDtypeStruct((kv_rows, MLA_HEADS * HEAD_PAD), BF16),
            jax.ShapeDtypeStruct((MLA_HEADS * V_DIM, kv_rows), BF16),
        ],
        compiler_params=_cparams(("arbitrary",)),
        name="proj_in",
    )(x_all, modv, ca, cb, wp["w_in"], wp["g_v"], wp["b_v"], wp["w_sp"], wp["b_sp"],
      wp["g_q"], wp["w_q"], wp["g_kv"], wp["w_k"], wp["w_vt"])


def _ssd_kernel(backward, n_ctx, n_lat, *refs):
    if backward:
        (xp_ref, xc_ref, xn_ref, dt_ref, cw_ref, cbias_ref, dtb_ref, alog_ref, e_ref,
         yf_ref, z_ref, skip_ref, gssm_ref, out_ref, h_scr) = refs
    else:
        (xp_ref, xc_ref, xn_ref, dt_ref, cw_ref, cbias_ref, dtb_ref, alog_ref, e_ref,
         out_ref, h_scr) = refs
    q = SSM_CHUNK
    s = pl.program_id(1)
    is_ctx = s < n_ctx
    if backward:
        cc = jnp.where(is_ctx, n_ctx - 1 - s, n_lat - 1 - (s - n_ctx))
    else:
        cc = jnp.where(is_ctx, s, s - n_ctx)
    ncc = jnp.where(is_ctx, n_ctx, n_lat)
    at_start = cc == 0
    at_end = cc == ncc - 1

    @pl.when(s == 0)
    def _():
        h_scr[...] = jnp.zeros_like(h_scr)

    prev = jnp.where(at_start, 0.0, xp_ref[...])
    nxt = jnp.where(at_end, 0.0, xn_ref[...])
    ext = jnp.concatenate([prev, xc_ref[...], nxt], axis=0)
    acc = jnp.broadcast_to(cbias_ref[...], (q, XBC_WIDTH))
    base = 8 - SSM_CONV // 2
    for j in range(SSM_CONV):
        acc = acc + cw_ref[j:j + 1, :] * ext[base + j:base + j + q, :]
    u = _silu(acc)
    xc = u[:, :SSM_INNER]

    lane = lax.broadcasted_iota(jnp.int32, (q, LANE), 1)
    dtr = dt_ref[...] + dtb_ref[...]
    dtv = jnp.maximum(dtr, 0.0) + jnp.log1p(jnp.exp(-jnp.abs(dtr)))
    dtv = jnp.where(lane < SSM_HEADS, dtv, 0.0)
    da = dtv * (-jnp.exp(alog_ref[...]))

    ri = lax.broadcasted_iota(jnp.int32, (q, q), 0)
    ci = lax.broadcasted_iota(jnp.int32, (q, q), 1)
    mask = (ci >= ri) if backward else (ci <= ri)
    tri = jnp.where(mask, 1.0, 0.0).astype(F32)
    acs = jnp.dot(tri, da, precision=HIGHEST, preferred_element_type=F32)
    acs_t = acs.T
    last = 0 if backward else q - 1
    acs_last = acs[last:last + 1, :]
    small = jnp.concatenate([dtv, jnp.exp(acs_last - acs), jnp.exp(acs)], axis=0)
    wide = jnp.dot(small, e_ref[...], precision=HIGHEST, preferred_element_type=F32)
    dt_w = wide[:q]
    decay_w = wide[q:2 * q]
    eacs_w = wide[2 * q:]
    elast_w = eacs_w[last:last + 1, :]

    xdt = xc * dt_w
    xw_b = (xdt * decay_w).astype(BF16)
    h_old = h_scr[...]
    h_b = h_old.astype(BF16)

    lane_w = lax.broadcasted_iota(jnp.int32, (q, SSM_INNER), 1) // SSM_HEAD_DIM
    w_parts = []
    x_parts = []
    inter = []
    upd = []
    per_g = SSM_HEADS // SSM_GROUPS
    gw = per_g * SSM_HEAD_DIM
    for g in range(SSM_GROUPS):
        b_g = u[:, SSM_INNER + g * SSM_STATE:SSM_INNER + (g + 1) * SSM_STATE].astype(BF16)
        o_c = SSM_INNER + SSM_GROUPS * SSM_STATE
        c_g = u[:, o_c + g * SSM_STATE:o_c + (g + 1) * SSM_STATE].astype(BF16)
        cbm = lax.dot_general(c_g, b_g, (((1,), (1,)), ((), ())), preferred_element_type=F32)
        for hh in range(per_g):
            hd = g * per_g + hh
            seg = acs[:, hd:hd + 1] - acs_t[hd:hd + 1, :]
            lmat = jnp.exp(jnp.where(mask, seg, -jnp.inf))
            w_parts.append((cbm * lmat).astype(BF16))
            x_parts.append(jnp.where(lane_w == hd, xdt, 0.0).astype(BF16))
        inter.append(jnp.dot(c_g, h_b[:, g * gw:(g + 1) * gw], preferred_element_type=F32))
        upd.append(lax.dot_general(b_g, xw_b[:, g * gw:(g + 1) * gw], (((0,), (0,)), ((), ())),
                                   preferred_element_type=F32))
    y = jnp.dot(jnp.concatenate(w_parts, axis=1), jnp.concatenate(x_parts, axis=0),
                preferred_element_type=F32)
    y = y + jnp.concatenate(inter, axis=1) * eacs_w
    h_scr[...] = h_old * elast_w + jnp.concatenate(upd, axis=1)

    if backward:
        tot = yf_ref[...] + y + skip_ref[...] * xc
        yz = tot * _silu(z_ref[...])
        half = SSM_INNER // SSM_GROUPS
        outs = []
        for g in range(SSM_GROUPS):
            grp = yz[:, g * half:(g + 1) * half]
            outs.append(grp * lax.rsqrt(jnp.mean(grp * grp, axis=-1, keepdims=True) + RMS_EPS))
        out_ref[...] = (jnp.concatenate(outs, axis=1) * gssm_ref[...]).astype(BF16)
    else:
        out_ref[...] = y


def _ssd_sweep(backward, xbc, dtc, wp, dims, yf=None, z=None):
    nb = dims["batch"]
    n_lat = dims["seq"] // SSM_CHUNK
    n_ctx = dims["ctx"] // SSM_CHUNK
    ctx_base = nb * n_lat
    rows = xbc.shape[0]
    sub = SSM_CHUNK // 8

    def blk(b, s):
        is_ctx = s < n_ctx
        if backward:
            cc = jnp.where(is_ctx, n_ctx - 1 - s, n_lat - 1 - (s - n_ctx))
        else:
            cc = jnp.where(is_ctx, s, s - n_ctx)
        return jnp.where(is_ctx, ctx_base + b * n_ctx + cc, b * n_lat + cc)

    cur = lambda w: pl.BlockSpec((SSM_CHUNK, w), lambda b, s: (blk(b, s), 0))
    d = 1 if backward else 0
    in_specs = [
        pl.BlockSpec((8, XBC_WIDTH), lambda b, s: (jnp.maximum(blk(b, s) * sub - 1, 0), 0)),
        cur(XBC_WIDTH),
        pl.BlockSpec((8, XBC_WIDTH), lambda b, s: (jnp.minimum((blk(b, s) + 1) * sub, rows // 8 - 1), 0)),
        pl.BlockSpec((SSM_CHUNK, LANE), lambda b, s: (blk(b, s), d)),
        _resident((8, XBC_WIDTH)), _resident((1, XBC_WIDTH)),
        _resident((1, LANE)), _resident((1, LANE)), _resident((LANE, SSM_INNER)),
    ]
    args = [xbc, xbc, xbc, dtc, wp["conv_w"], wp["conv_b"], wp["dt_bias"][d], wp["a_log"][d], wp["expand"]]
    if backward:
        in_specs += [cur(SSM_INNER), cur(SSM_INNER), _resident((1, SSM_INNER)), _resident((1, SSM_INNER))]
        args += [yf, z, wp["d_skip"], wp["g_ssm"]]
    return pl.pallas_call(
        functools.partial(_ssd_kernel, backward, n_ctx, n_lat),
        grid=(nb, n_ctx + n_lat),
        in_specs=in_specs,
        out_specs=cur(SSM_INNER),
        out_shape=jax.ShapeDtypeStruct((rows, SSM_INNER), BF16 if backward else F32),
        scratch_shapes=[pltpu.VMEM((SSM_STATE, SSM_INNER), F32)],
        compiler_params=_cparams(("arbitrary", "arbitrary")),
        name="ssd_bwd" if backward else "ssd_fwd",
    )(*args)


def _attn_kernel(nk, tk, q_ref, k_ref, vt_ref, o_ref):
    q = q_ref[...]
    tq = q.shape[0]
    m = jnp.full((1, tq), -jnp.inf, F32)
    l = jnp.zeros((1, tq), F32)
    acc = jnp.zeros((V_DIM, tq), F32)
    for i in range(nk):
        k = k_ref[i * tk:(i + 1) * tk, :]
        vt = vt_ref[:, i * tk:(i + 1) * tk]
        st = lax.dot_general(k, q, (((1,), (1,)), ((), ())), preferred_element_type=F32)
        m_new = jnp.maximum(m, jnp.max(st, axis=0, keepdims=True))
        alpha = jnp.exp2(m - m_new)
        p = jnp.exp2(st - m_new)
        l = alpha * l + jnp.sum(p, axis=0, keepdims=True)
        acc = alpha * acc + jnp.dot(vt, p.astype(BF16), preferred_element_type=F32)
        m = m_new
    o_ref[...] = (acc / l).T.astype(BF16)


def _attention(q, k, vt, nb, lq, lk, q_row0, kv_stride, tq, tk):
    nq = lq // tq
    q0 = q_row0 // tq
    assert kv_stride % lk == 0 and lk % tk == 0 and q_row0 % tq == 0
    kvb = kv_stride // lk
    return pl.pallas_call(
        functools.partial(_attn_kernel, lk // tk, tk),
        grid=(nb, MLA_HEADS, nq),
        in_specs=[
            pl.BlockSpec((tq, HEAD_PAD), lambda b, h, i: (q0 + b * nq + i, h)),
            pl.BlockSpec((lk, HEAD_PAD), lambda b, h, i: (b * kvb, h)),
            pl.BlockSpec((V_DIM, lk), lambda b, h, i: (h, b * kvb)),
        ],
        out_specs=pl.BlockSpec((tq, V_DIM), lambda b, h, i: (b * nq + i, h)),
        out_shape=jax.ShapeDtypeStruct((nb * lq, MLA_HEADS * V_DIM), BF16),
        compiler_params=_cparams(("arbitrary", "arbitrary", "arbitrary")),
        name="mla_attention",
    )(q, k, vt)


def _out_proj_kernel(n_lat_tiles, gm_ref, ssm_ref, att_ref, attc_ref, x_ref, mod_ref, w_ref, g_ref, b_ref, wr_ref,
                     x1_ref, tok_ref, lg_ref):
    mod = mod_ref[0]
    for r0 in range(0, x_ref.shape[0], ROW_TILE):
        rows = slice(r0, r0 + ROW_TILE)
        att = att_ref[rows, :]
        if attc_ref is not None:
            att = jnp.where(pl.program_id(0) >= n_lat_tiles, attc_ref[rows, :], att)
        mixed = jnp.concatenate([gm_ref[rows, :], ssm_ref[rows, :], att], axis=1)
        mix = jnp.dot(mixed, w_ref[...], preferred_element_type=F32)
        x1 = _layer_norm(DN_ALPHA * x_ref[rows, :] + mod[2:3, :] * mix, g_ref[...], b_ref[...])
        x1_ref[rows, :] = x1
        tok = x1 * (1.0 + mod[4:5, :]) + mod[3:4, :]
        tok_ref[rows, :] = _pack_rows(tok)
        lg_ref[:, rows] = lax.dot_general(wr_ref[...], tok, (((1,), (1,)), ((), ())),
                                          precision=HIGHEST, preferred_element_type=F32)


def _out_proj_lat_kernel(n_lat_tiles, gm_ref, ssm_ref, att_ref, *rest):
    _out_proj_kernel(n_lat_tiles, gm_ref, ssm_ref, att_ref, None, *rest)


def _out_proj(gm, ssm, att, att_ctx, x_all, modv, wp, dims, rows):
    tm = OUT_TILE
    assert dims["seq"] % tm == 0 and (dims["batch"] * dims["ctx"]) % tm == 0
    lat_tiles = dims["seq"] // tm
    nb = dims["batch"]
    n_lat = nb * lat_tiles
    row = lambda w: pl.BlockSpec((tm, w), lambda i: (i, 0))
    att_specs = [pl.BlockSpec((tm, MLA_HEADS * V_DIM), lambda i: (jnp.minimum(i, n_lat - 1), 0))]
    att_args = [att]
    body = _out_proj_lat_kernel
    if att_ctx is not None:
        att_specs.append(pl.BlockSpec((tm, MLA_HEADS * V_DIM), lambda i: (jnp.maximum(i - n_lat, 0), 0)))
        att_args.append(att_ctx)
        body = _out_proj_kernel
    return pl.pallas_call(
        functools.partial(body, n_lat),
        grid=(rows // tm,),
        in_specs=[
            row(GM_WIDTH), row(SSM_INNER), *att_specs, row(D_MODEL),
            pl.BlockSpec((1, 6, D_MODEL), lambda i: (jnp.minimum(i // lat_tiles, nb), 0, 0)),
            _resident((D_MODEL, D_MODEL)), _resident((1, D_MODEL)), _resident((1, D_MODEL)),
            _resident((N_EXPERTS, D_MODEL)),
        ],
        out_specs=[row(D_MODEL), row(D_MODEL // 2), pl.BlockSpec((N_EXPERTS, tm), lambda i: (0, i))],
        out_shape=[
            jax.ShapeDtypeStruct((rows, D_MODEL), F32),
            jax.ShapeDtypeStruct((rows, D_MODEL // 2), jnp.uint32),
            jax.ShapeDtypeStruct((N_EXPERTS, rows), F32),
        ],
        compiler_params=_cparams(("arbitrary",)),
        name="out_proj",
    )(gm, ssm, *att_args, x_all, modv, wp["w_out"], wp["ln1_g"], wp["ln1_b"], wp["w_router_t"])


def _gate_kernel(lg_ref, b_ref, eidx_ref, w_ref, pos_ref, cnt_ref, carry_scr):
    per = N_EXPERTS // N_EXPERT_GROUPS
    tn = lg_ref.shape[1]
    sc = jax.nn.sigmoid(lg_ref[...])
    bi = sc + b_ref[...]
    neg = np.float32(-np.inf)
    xs = [bi[j * N_EXPERT_GROUPS:(j + 1) * N_EXPERT_GROUPS, :] for j in range(per)]
    ss = [sc[j * N_EXPERT_GROUPS:(j + 1) * N_EXPERT_GROUPS, :] for j in range(per)]
    m1 = functools.reduce(jnp.maximum, xs)
    first = functools.reduce(jnp.minimum, [jnp.where(xs[j] == m1, j, per) for j in range(per)])
    m2 = functools.reduce(jnp.maximum, [jnp.where(first == j, neg, xs[j]) for j in range(per)])
    gscore = m1 + m2
    gi = lax.broadcasted_iota(jnp.int32, (N_EXPERT_GROUPS, tn), 0)
    cur = gscore
    gsel = jnp.zeros((N_EXPERT_GROUPS, tn), jnp.int32)
    for _ in range(TOPK_GROUPS):
        mx = jnp.max(cur, axis=0, keepdims=True)
        fi = jnp.min(jnp.where(cur == mx, gi, N_EXPERT_GROUPS), axis=0, keepdims=True)
        pick = gi == fi
        gsel = jnp.where(pick, 1, gsel)
        cur = jnp.where(pick, neg, cur)
    cur = [jnp.where(gsel > 0, xs[j], neg) for j in range(per)]
    eid = [gi * per + j for j in range(per)]
    sel = [jnp.zeros((N_EXPERT_GROUPS, tn), F32) for _ in range(per)]
    picked = []
    for _ in range(TOP_K):
        mx = jnp.max(functools.reduce(jnp.maximum, cur), axis=0, keepdims=True)
        cand = functools.reduce(jnp.minimum, [jnp.where(cur[j] == mx, eid[j], N_EXPERTS) for j in range(per)])
        fi = jnp.min(cand, axis=0, keepdims=True)
        wk = jnp.zeros((N_EXPERT_GROUPS, tn), F32)
        for j in range(per):
            pick = eid[j] == fi
            sel[j] = jnp.where(pick, 1.0, sel[j])
            cur[j] = jnp.where(pick, neg, cur[j])
            wk = wk + jnp.where(pick, ss[j], 0.0)
        picked.append((fi, jnp.sum(wk, axis=0, keepdims=True)))
    tot = functools.reduce(jnp.add, [wk for _, wk in picked])

    @pl.when(pl.program_id(0) == 0)
    def _():
        carry_scr[...] = jnp.zeros_like(carry_scr)

    sel_all = jnp.concatenate(sel, axis=0)
    ri = lax.broadcasted_iota(jnp.int32, (tn, tn), 0)
    ci = lax.broadcasted_iota(jnp.int32, (tn, tn), 1)
    before = jnp.where(ri < ci, 1.0, 0.0).astype(BF16)
    prefix = jnp.dot(sel_all.astype(BF16), before, preferred_element_type=F32)
    carry = carry_scr[...]
    rank = prefix + carry
    carry_new = carry + prefix[:, tn - 1:tn] + sel_all[:, tn - 1:tn]
    carry_scr[...] = carry_new
    cnt_ref[...] = jnp.broadcast_to(carry_new, cnt_ref.shape).astype(jnp.int32)
    for k, (fi, wk) in enumerate(picked):
        pk = jnp.zeros((N_EXPERT_GROUPS, tn), F32)
        for j in range(per):
            pk = pk + jnp.where(eid[j] == fi, rank[j * N_EXPERT_GROUPS:(j + 1) * N_EXPERT_GROUPS, :], 0.0)
        eidx_ref[k:k + 1, :] = fi
        w_ref[k:k + 1, :] = wk / tot * ROUTED_SCALE
        pos_ref[k:k + 1, :] = jnp.sum(pk, axis=0, keepdims=True).astype(jnp.int32)


def _gate(logits_t, b_router_col):
    t = logits_t.shape[1]
    tn = 512
    tile = pl.BlockSpec((TOP_K, tn), lambda i: (0, i))
    return pl.pallas_call(
        _gate_kernel,
        grid=(t // tn,),
        in_specs=[pl.BlockSpec((N_EXPERTS, tn), lambda i: (0, i)), _resident((N_EXPERTS, 1))],
        out_specs=[tile, tile, tile, pl.BlockSpec((N_EXPERTS, LANE), lambda i: (0, 0))],
        out_shape=[
            jax.ShapeDtypeStruct((TOP_K, t), jnp.int32),
            jax.ShapeDtypeStruct((TOP_K, t), F32),
            jax.ShapeDtypeStruct((TOP_K, t), jnp.int32),
            jax.ShapeDtypeStruct((N_EXPERTS, LANE), jnp.int32),
        ],
        scratch_shapes=[pltpu.VMEM((N_EXPERTS, 1), F32)],
        compiler_params=_cparams(("arbitrary",)),
        name="gate_topk",
    )(logits_t, b_router_col)


DISPATCH_TILE = 512


def _dispatch_kernel(padlo_ref, padn_ref, nu_ref, slot_ref, tok_ref, xs_hbm, zero_buf, sem, zsem):
    i = pl.program_id(0)
    td = slot_ref.shape[1]

    def row_copy(r, k):
        return pltpu.make_async_copy(tok_ref.at[pl.ds(r, 1)], xs_hbm.at[pl.ds(slot_ref[k, r], 1)], sem)

    def issue(r, carry):
        for k in range(TOP_K):
            row_copy(r, k).start()
        return carry

    lax.fori_loop(0, td, issue, 0)

    @pl.when(i == 0)
    def _():
        zero_buf[...] = jnp.zeros_like(zero_buf)

        def fill(e, carry):
            lo = padlo_ref[e]
            n = padn_ref[e]
            head = jnp.minimum((-lo) & 7, n)

            for s in range(7):
                @pl.when(s < head)
                def _(s=s):
                    cp = pltpu.make_async_copy(zero_buf.at[pl.ds(0, 1)], xs_hbm.at[pl.ds(lo + s, 1)], zsem)
                    cp.start()
                    cp.wait()

            rest = n - head
            off = lo + head
            size = EXPERT_BLOCK // 2
            while size >= 8:
                hit = (rest & size) != 0

                @pl.when(hit)
                def _(off=off, size=size):
                    dst = xs_hbm.at[pl.ds(pl.multiple_of(off, 8), size)]
                    cp = pltpu.make_async_copy(zero_buf.at[pl.ds(0, size)], dst, zsem)
                    cp.start()
                    cp.wait()

                off = off + jnp.where(hit, size, 0)
                size //= 2
            return carry

        lax.fori_loop(0, N_EXPERTS, fill, 0)

        zrows = zero_buf.shape[0]

        def fill_tail(b, carry):
            for h in range(EXPERT_BLOCK // zrows):
                row0 = pl.multiple_of(b * EXPERT_BLOCK + h * zrows, 8)
                cp = pltpu.make_async_copy(zero_buf, xs_hbm.at[pl.ds(row0, zrows)], zsem)
                cp.start()
                cp.wait()
            return carry

        lax.fori_loop(nu_ref[0], xs_hbm.shape[0] // EXPERT_BLOCK, fill_tail, 0)

    for _ in range(TOP_K):
        pltpu.make_async_copy(tok_ref, xs_hbm.at[pl.ds(0, td)], sem).wait()


def _dispatch_rows(slot, pad_lo, pad_n, n_used, tok, cap):
    t = tok.shape[0]
    td = DISPATCH_TILE
    return pl.pallas_call(
        _dispatch_kernel,
        grid_spec=pltpu.PrefetchScalarGridSpec(
            num_scalar_prefetch=3,
            grid=(t // td,),
            in_specs=[
                pl.BlockSpec((TOP_K, td), lambda i, lo, n, nu: (0, i), memory_space=pltpu.SMEM),
                pl.BlockSpec((td, D_MODEL // 2), lambda i, lo, n, nu: (i, 0)),
            ],
            out_specs=pl.BlockSpec(memory_space=pl.ANY),
            scratch_shapes=[pltpu.VMEM((EXPERT_BLOCK // 2, D_MODEL // 2), jnp.uint32),
                            pltpu.SemaphoreType.DMA, pltpu.SemaphoreType.DMA],
        ),
        out_shape=jax.ShapeDtypeStruct((cap, D_MODEL // 2), jnp.uint32),
        compiler_params=_cparams(("arbitrary",)),
        name="dispatch_rows",
    )(pad_lo, pad_n, n_used, slot, tok)


def _experts_kernel(be_ref, nu_ref, x_ref, wg_ref, wu_ref, wd_ref, y_ref, wg_s, wu_s, wd_s):
    j = pl.program_id(0)
    prev = be_ref[jnp.maximum(j - 1, 0)]
    fresh = jnp.logical_or(j == 0, be_ref[j] != prev)

    @pl.when(fresh)
    def _():
        wg_s[...] = wg_ref[0].astype(BF16)
        wu_s[...] = wu_ref[0].astype(BF16)
        wd_s[...] = wd_ref[0].astype(BF16)

    @pl.when(j < nu_ref[0])
    def _():
        x = _unpack_rows(x_ref[...])
        g = jnp.dot(x, wg_s[...], preferred_element_type=F32)
        u = jnp.dot(x, wu_s[...], preferred_element_type=F32)
        hb = (_silu(g) * u).astype(BF16)
        y_ref[...] = _pack_rows(jnp.dot(hb, wd_s[...], preferred_element_type=F32))

    @pl.when(j >= nu_ref[0])
    def _():
        y_ref[...] = jnp.zeros_like(y_ref)


def _experts(layer, block_e, n_used, x_sorted, w_gate, w_up, w_down):
    cap = x_sorted.shape[0]
    blk = EXPERT_BLOCK
    return pl.pallas_call(
        _experts_kernel,
        grid_spec=pltpu.PrefetchScalarGridSpec(
            num_scalar_prefetch=2,
            grid=(cap // blk,),
            in_specs=[
                pl.BlockSpec((blk, D_MODEL // 2), lambda j, be, nu: (jnp.minimum(j, nu[0] - 1), 0)),
                pl.BlockSpec((None, 1, D_MODEL, EXPERT_FF), lambda j, be, nu: (layer, be[j], 0, 0)),
                pl.BlockSpec((None, 1, D_MODEL, EXPERT_FF), lambda j, be, nu: (layer, be[j], 0, 0)),
                pl.BlockSpec((None, 1, EXPERT_FF, D_MODEL), lambda j, be, nu: (layer, be[j], 0, 0)),
            ],
            out_specs=pl.BlockSpec((blk, D_MODEL // 2), lambda j, be, nu: (j, 0)),
            scratch_shapes=[pltpu.VMEM((D_MODEL, EXPERT_FF), BF16), pltpu.VMEM((D_MODEL, EXPERT_FF), BF16),
                            pltpu.VMEM((EXPERT_FF, D_MODEL), BF16)],
        ),
        out_shape=jax.ShapeDtypeStruct((cap, D_MODEL // 2), jnp.uint32),
        compiler_params=_cparams(("arbitrary",)),
        name="routed_experts",
    )(block_e, n_used, x_sorted, w_gate, w_up, w_down)


COMBINE_TILE = 256


def _final_kernel(slot_ref, slot_next_ref, w8_ref, tok_ref, x1_ref, mod_ref, wgu_ref, wd_ref, g_ref, b_ref,
                  y_hbm, o_ref, ybuf, sems):
    i = pl.program_id(0)
    n = pl.num_programs(0)
    tf = tok_ref.shape[0]
    cur = i % 2

    def gather(slots, buf):
        def issue(r, carry):
            for k in range(TOP_K):
                pltpu.make_async_copy(y_hbm.at[pl.ds(slots[k, r], 1)], ybuf.at[buf, k, pl.ds(r, 1)],
                                      sems.at[buf]).start()
            return carry

        lax.fori_loop(0, tf, issue, 0)

    @pl.when(i == 0)
    def _():
        gather(slot_ref, 0)

    @pl.when(i + 1 < n)
    def _():
        gather(slot_next_ref, 1 - cur)

    mod = mod_ref[0]
    gu = jnp.dot(_unpack_rows(tok_ref[...]), wgu_ref[...], preferred_element_type=F32)
    hb = (_silu(gu[:, :SHARED_FF]) * gu[:, SHARED_FF:]).astype(BF16)
    f = jnp.dot(hb, wd_ref[...], preferred_element_type=F32)

    for k in range(TOP_K):
        pltpu.make_async_copy(y_hbm.at[pl.ds(0, tf)], ybuf.at[cur, k], sems.at[cur]).wait()
    w8 = w8_ref[...]
    half = D_MODEL // 2
    f_lo = f[:, :half]
    f_hi = f[:, half:]
    for k in range(TOP_K):
        yk = ybuf[cur, k]
        wk = w8[:, k:k + 1]
        f_lo = f_lo + wk * pltpu.bitcast(yk << 16, F32)
        f_hi = f_hi + wk * pltpu.bitcast(yk & jnp.uint32(0xFFFF0000), F32)
    f = jnp.concatenate([f_lo, f_hi], axis=1)
    o_ref[...] = _layer_norm(DN_ALPHA * x1_ref[...] + mod[5:6, :] * f, g_ref[...], b_ref[...])


def _final(slot, w8_t, tok, y_sorted, x1, modv, wp, dims, rows):
    tf = COMBINE_TILE
    nt = rows // tf
    lat_tiles = dims["seq"] // tf
    nb = dims["batch"]
    row = lambda w: pl.BlockSpec((tf, w), lambda i: (i, 0))
    return pl.pallas_call(
        _final_kernel,
        grid=(nt,),
        in_specs=[
            pl.BlockSpec((TOP_K, tf), lambda i: (0, i), memory_space=pltpu.SMEM),
            pl.BlockSpec((TOP_K, tf), lambda i: (0, jnp.minimum(i + 1, nt - 1)), memory_space=pltpu.SMEM),
            row(TOP_K), row(D_MODEL // 2), row(D_MODEL),
            pl.BlockSpec((1, 6, D_MODEL), lambda i: (jnp.minimum(i // lat_tiles, nb), 0, 0)),
            _resident((D_MODEL, 2 * SHARED_FF)), _resident((SHARED_FF, D_MODEL)),
            _resident((1, D_MODEL)), _resident((1, D_MODEL)),
            pl.BlockSpec(memory_space=pl.ANY),
        ],
        out_specs=row(D_MODEL),
        out_shape=jax.ShapeDtypeStruct((rows, D_MODEL), F32),
        scratch_shapes=[pltpu.VMEM((2, TOP_K, tf, D_MODEL // 2), jnp.uint32), pltpu.SemaphoreType.DMA((2,))],
        compiler_params=_cparams(("arbitrary",)),
        name="combine_shared_ln2",
    )(slot, slot, w8_t, tok, x1, modv, wp["w_sh_gu"], wp["w_sh_down"], wp["ln2_g"], wp["ln2_b"], y_sorted)


def _rot_half_cols():
    src = np.zeros((QK_ROPE,), np.int32)
    sign = np.zeros((QK_ROPE,), np.float32)
    quarter = QK_ROPE // 4
    for dcol in range(QK_ROPE):
        part, i = divmod(dcol, 2 * quarter)
        half, kk = divmod(i, quarter)
        src[dcol] = part * 2 * quarter + (quarter + kk if half == 0 else kk)
        sign[dcol] = -1.0 if half == 0 else 1.0
    return src, sign


def _expert_major(a, axis):
    per = N_EXPERTS // N_EXPERT_GROUPS
    shp = a.shape
    a = a.reshape(shp[:axis] + (N_EXPERT_GROUPS, per) + shp[axis + 1:])
    a = jnp.swapaxes(a, axis, axis + 1)
    return a.reshape(shp)


def _prep_layer(l, p):
    src, sign = _rot_half_cols()
    w_in = p["w_in"][l]
    col = lambda lo, n: w_in[:, lo:lo + n]
    zpad = jnp.zeros((D_MODEL, LANE - SSM_HEADS), F32)
    w_kr = col(R_KR, QK_ROPE)
    w_in_p = jnp.concatenate([
        col(R_GM, 2 * GM_WIDTH), col(R_Q, Q_LORA), col(R_Z, SSM_INNER), col(R_XBC, XBC_WIDTH),
        col(R_DT, SSM_HEADS), zpad, col(R_DT + SSM_HEADS, SSM_HEADS), zpad,
        col(R_KV, KV_LORA), w_kr, w_kr[:, src] * sign,
    ], axis=1).astype(BF16)
    wq = p["w_q_b"][l].reshape(Q_LORA, MLA_HEADS, QK_NOPE + QK_ROPE)
    wq_rope = wq[:, :, QK_NOPE:]
    w_q = jnp.concatenate([wq[:, :, :QK_NOPE], wq_rope, wq_rope[:, :, src] * sign], axis=2)
    w_q = w_q.reshape(Q_LORA, MLA_HEADS * HEAD_PAD).astype(BF16)
    wkv = p["w_kv_b"][l].reshape(KV_LORA, MLA_HEADS, QK_NOPE + V_DIM)
    w_k = wkv[:, :, :QK_NOPE].reshape(KV_LORA, -1).astype(BF16)
    w_vt = wkv[:, :, QK_NOPE:].reshape(KV_LORA, -1).T.astype(BF16)
    lane_pad = lambda a: jnp.pad(a, ((0, 0), (0, LANE - a.shape[1])))
    expand = jnp.repeat(jnp.eye(LANE, SSM_HEADS, dtype=F32), SSM_HEAD_DIM, axis=1)
    r2 = lambda a: a.reshape(1, -1)
    return dict(
        w_in=w_in_p, g_v=r2(p["g_v"][l]), b_v=r2(p["b_v"][l]), w_sp=p["w_sp"][l].astype(BF16),
        b_sp=jnp.repeat(p["b_sp"][l].T, GM_CHUNK, axis=1),
        g_q=r2(p["g_q"][l]), w_q=w_q, g_kv=r2(p["g_kv"][l]), w_k=w_k, w_vt=w_vt,
        conv_w=jnp.pad(p["conv_w"][l], ((0, 8 - SSM_CONV), (0, 0))), conv_b=r2(p["conv_b"][l]),
        dt_bias=lane_pad(p["dt_bias"][l])[:, None, :], a_log=lane_pad(p["a_log"][l])[:, None, :],
        expand=expand, d_skip=r2(jnp.repeat(p["d_skip"][l], SSM_HEAD_DIM)), g_ssm=r2(p["g_ssm"][l]),
        w_out=p["w_out"][l].astype(BF16), ln1_g=r2(p["ln1_g"][l]), ln1_b=r2(p["ln1_b"][l]),
        ln2_g=r2(p["ln2_g"][l]), ln2_b=r2(p["ln2_b"][l]),
        w_router_t=_expert_major(p["w_router"][l].T, 0),
        b_router=_expert_major(p["b_router"][l], 0).reshape(N_EXPERTS, 1),
        w_sh_gu=jnp.concatenate([p["w_sh_gate"][l], p["w_sh_up"][l]], axis=1).astype(BF16),
        w_sh_down=p["w_sh_down"][l].astype(BF16),
    )


def _rope_tables(nb, seq, nctx):
    t = jnp.arange(seq)
    half = QK_ROPE // 2
    inv = 1.0 / (ROPE_BASE ** (jnp.arange(0, half, 2, dtype=F32) / half))
    ang_r = (t // GRID_W)[:, None] * inv
    ang_c = (t % GRID_W)[:, None] * inv
    ang = jnp.concatenate([ang_r, ang_r, ang_c, ang_c], axis=1)
    zero = jnp.zeros((seq, LANE - QK_ROPE), F32)
    ca_lat = jnp.concatenate([jnp.cos(ang), zero], axis=1)
    cb_lat = jnp.concatenate([jnp.sin(ang), zero], axis=1)
    ca_ctx = jnp.concatenate([jnp.ones((nb * nctx, QK_ROPE), F32), jnp.zeros((nb * nctx, LANE - QK_ROPE), F32)], axis=1)
    ca = jnp.concatenate([jnp.tile(ca_lat, (nb, 1)), ca_ctx], axis=0)
    cb = jnp.concatenate([jnp.tile(cb_lat, (nb, 1)), jnp.zeros((nb * nctx, LANE), F32)], axis=0)
    return ca, cb


def _slot_layout(eidx, pos, cnt, t):
    per = N_EXPERTS // N_EXPERT_GROUPS
    blk = EXPERT_BLOCK
    counts = cnt[:, 0].reshape(per, N_EXPERT_GROUPS).T.reshape(N_EXPERTS)
    padded = (counts + blk - 1) // blk * blk
    pend = jnp.cumsum(padded)
    pstart = pend - padded
    n_blocks = (t * TOP_K + N_EXPERTS * (blk - 1) + blk - 1) // blk
    block_e = jnp.sum((pend[None, :] <= (jnp.arange(n_blocks) * blk)[:, None]).astype(jnp.int32), axis=1)
    block_e = jnp.minimum(block_e, N_EXPERTS - 1)
    n_used = (pend[-1] // blk).reshape(1)
    hit = eidx[:, :, None] == jnp.arange(N_EXPERTS, dtype=jnp.int32)
    slot = (jnp.sum(jnp.where(hit, pstart, 0), axis=2) + pos).astype(jnp.int32)
    return (slot, block_e.astype(jnp.int32), n_used.astype(jnp.int32), (pstart + counts).astype(jnp.int32),
            (padded - counts).astype(jnp.int32), n_blocks * blk)


def kernel(x, c, ctx, c_ctx, w_mod, b_mod, w_in, g_q, w_q_b, g_kv, w_kv_b, conv_w, conv_b, a_log, dt_bias, d_skip, g_ssm, g_v, b_v, w_sp, b_sp, w_out, ln1_g, ln1_b, ln2_g, ln2_b, w_router, b_router, w_e_gate, w_e_up, w_e_down, w_sh_gate, w_sh_up, w_sh_down):
    p = dict(w_in=w_in, g_q=g_q, w_q_b=w_q_b, g_kv=g_kv, w_kv_b=w_kv_b, conv_w=conv_w, conv_b=conv_b,
             a_log=a_log, dt_bias=dt_bias, d_skip=d_skip, g_ssm=g_ssm, g_v=g_v, b_v=b_v, w_sp=w_sp, b_sp=b_sp,
             w_out=w_out, ln1_g=ln1_g, ln1_b=ln1_b, ln2_g=ln2_g, ln2_b=ln2_b, w_router=w_router,
             b_router=b_router, w_sh_gate=w_sh_gate, w_sh_up=w_sh_up, w_sh_down=w_sh_down)
    nb, seq, d = x.shape
    nctx = ctx.shape[1]
    depth = w_mod.shape[0]
    assert d == D_MODEL and depth == DEPTH and nb + 1 <= 8
    assert seq % ROW_TILE == 0 and nctx % ROW_TILE == 0 and seq % GRID_W == 0
    dims = dict(batch=nb, seq=seq, ctx=nctx)
    lat_rows = nb * seq
    rows_all = lat_rows + nb * nctx
    lk = seq + nctx

    x_all = jnp.concatenate([x.reshape(lat_rows, d), ctx.reshape(nb * nctx, d)], axis=0)
    cvec = jnp.zeros((8, d), F32).at[:nb].set(c).at[nb].set(c_ctx)
    mods = _modulation(cvec, w_mod, b_mod).reshape(depth, 8, 6, d)[:, :nb + 1]
    ca, cb = _rope_tables(nb, seq, nctx)
    tq = min(1024, seq)
    tk = 768 if lk % 768 == 0 else ROW_TILE

    for l in range(depth):
        last = l == depth - 1
        wp = _prep_layer(l, p)
        modv = mods[l]
        gm, z, xbc, dtc, q, k, vt = _proj_in(x_all, modv, ca, cb, wp, dims)
        yf = _ssd_sweep(False, xbc, dtc, wp, dims)
        ssm = _ssd_sweep(True, xbc, dtc, wp, dims, yf=yf, z=z)
        att = _attention(q, k, vt, nb, seq, lk, 0, lk, tq, tk)
        rows = lat_rows if last else rows_all
        att_ctx = None if last else _attention(q, k, vt, nb, nctx, nctx, lat_rows, lk, ROW_TILE, ROW_TILE)
        x1, tok, logits_t = _out_proj(gm, ssm, att, att_ctx, x_all, modv, wp, dims, rows)
        eidx, w8, pos, cnt = _gate(logits_t, wp["b_router"])
        slot, block_e, n_used, pad_lo, pad_n, cap = _slot_layout(eidx, pos, cnt, rows)
        x_sorted = _dispatch_rows(slot, pad_lo, pad_n, n_used, tok, cap)
        y_sorted = _experts(l, block_e, n_used, x_sorted, w_e_gate, w_e_up, w_e_down)
        x_all = _final(slot, w8.T, tok, y_sorted, x1, modv, wp, dims, rows)
    return x_all.reshape(nb, seq, d)
```

```python
import functools
import math

import jax
import jax.numpy as jnp
import numpy as np
from jax import lax
from jax.experimental import pallas as pl
from jax.experimental.pallas import tpu as pltpu

F32 = jnp.float32
BF16 = jnp.bfloat16
HIGHEST = lax.Precision.HIGHEST

D_MODEL = 2048
GRID_W = 64
GM_HEADS = 4
GM_WIDTH = 512
GM_CHUNK = 128
SSM_HEADS = 8
SSM_HEAD_DIM = 64
SSM_INNER = 512
SSM_GROUPS = 2
SSM_STATE = 128
SSM_CONV = 5
SSM_CHUNK = 128
XBC_WIDTH = 1024
MLA_HEADS = 8
QK_NOPE = 128
QK_ROPE = 64
V_DIM = 128
Q_LORA = 768
KV_LORA = 256
ROPE_BASE = 10000.0
N_EXPERTS = 64
TOP_K = 8
N_EXPERT_GROUPS = 8
TOPK_GROUPS = 4
EXPERT_FF = 512
SHARED_FF = 512
ROUTED_SCALE = 2.5
LN_EPS = 1e-5
RMS_EPS = 1e-6
DEPTH = 2
DN_ALPHA = (2 * DEPTH) ** 0.25

R_GM = 0
R_Q = R_GM + 2 * GM_WIDTH
R_Z = R_Q + Q_LORA
R_XBC = R_Z + SSM_INNER
R_DT = R_XBC + XBC_WIDTH
R_KV = R_DT + 2 * SSM_HEADS
R_KR = R_KV + KV_LORA

LANE = 128
P_GM = 0
P_Q = P_GM + 2 * GM_WIDTH
P_Z = P_Q + Q_LORA
P_XBC = P_Z + SSM_INNER
P_DT = P_XBC + XBC_WIDTH
P_KV = P_DT + 2 * LANE
P_KR = P_KV + KV_LORA
P_WIDTH = P_KR + LANE

HEAD_PAD = 256
ROW_TILE = 256
OUT_TILE = 2 * ROW_TILE
SSD_STEP = 2
EXPERT_BLOCK = 512
VMEM_LIMIT = 56 * 1024 * 1024


def _cparams(sem):
    return pltpu.CompilerParams(dimension_semantics=sem, vmem_limit_bytes=VMEM_LIMIT)


def _resident(shape):
    n = len(shape)
    return pl.BlockSpec(shape, lambda *_: (0,) * n, pipeline_mode=pl.Buffered(1))


def _silu(x):
    return x * jax.nn.sigmoid(x)


def _layer_norm(y, g, b):
    mu = jnp.mean(y, axis=-1, keepdims=True)
    yc = y - mu
    var = jnp.mean(yc * yc, axis=-1, keepdims=True)
    return yc * lax.rsqrt(var + LN_EPS) * g + b


def _rms_norm(y, g):
    return y * lax.rsqrt(jnp.mean(y * y, axis=-1, keepdims=True) + RMS_EPS) * g


def _dot_01(a, b, ones_on_right=False):
    val, sel = (a, b) if ones_on_right else (b, a)
    out = None
    for _ in range(3):
        piece = val.astype(BF16)
        term = (jnp.dot(piece, sel, preferred_element_type=F32) if ones_on_right
                else jnp.dot(sel, piece, preferred_element_type=F32))
        out = term if out is None else out + term
        val = val - piece.astype(F32)
    return out


def _pack_rows(t):
    n = t.shape[1] // 2
    bits = pltpu.bitcast(t.astype(BF16).astype(F32), jnp.uint32)
    return (bits[:, :n] >> 16) | bits[:, n:]


def _unpack_rows(w):
    lo = pltpu.bitcast(w << 16, F32).astype(BF16)
    hi = pltpu.bitcast(w & jnp.uint32(0xFFFF0000), F32).astype(BF16)
    return jnp.concatenate([lo, hi], axis=1)


def _mod_kernel(c_ref, w_ref, b_ref, o_ref):
    c = c_ref[...]
    o_ref[0] = jnp.dot(_silu(c), w_ref[0], precision=HIGHEST, preferred_element_type=F32) + b_ref[0]


def _modulation(cvec, w_mod, b_mod):
    depth, d, n = w_mod.shape
    tn = 1536
    return pl.pallas_call(
        _mod_kernel,
        grid=(depth, n // tn),
        in_specs=[
            pl.BlockSpec((8, d), lambda l, j: (0, 0)),
            pl.BlockSpec((1, d, tn), lambda l, j: (l, 0, j)),
            pl.BlockSpec((1, 1, tn), lambda l, j: (l, 0, j)),
        ],
        out_specs=pl.BlockSpec((1, 8, tn), lambda l, j: (l, 0, j)),
        out_shape=jax.ShapeDtypeStruct((depth, 8, n), F32),
        compiler_params=_cparams(("arbitrary", "arbitrary")),
        name="modulation",
    )(cvec, w_mod, b_mod.reshape(depth, 1, n))


def _proj_in_kernel(x_ref, mod_ref, ca_ref, cb_ref, w_ref, gv_ref, bv_ref, wsp_ref, bsp_ref,
                    gq_ref, wq_ref, gkv_ref, wk_ref, wvt_ref,
                    gm_ref, z_ref, xbc_ref, dt_ref, q_ref, k_ref, vt_ref):
    tm = x_ref.shape[0]
    mod = mod_ref[0]
    h = (x_ref[...] * (1.0 + mod[1:2, :]) + mod[0:1, :]).astype(BF16)

    def proj(lo, n):
        return jnp.dot(h, w_ref[:, lo:lo + n], preferred_element_type=F32)

    uv = proj(P_GM, 2 * GM_WIDTH)
    gl = 0.5 * uv * (1.0 + lax.erf(uv * np.float32(math.sqrt(0.5))))
    u = gl[:, :GM_WIDTH]
    vn = _layer_norm(gl[:, GM_WIDTH:], gv_ref[...], bv_ref[...]).astype(BF16)
    for c in range(tm // GM_CHUNK):
        r0 = c * GM_CHUNK
        parts = [jnp.dot(wsp_ref[hh], vn[r0:r0 + GM_CHUNK, hh * LANE:(hh + 1) * LANE],
                         preferred_element_type=F32) for hh in range(GM_HEADS)]
        s = jnp.concatenate(parts, axis=1) + bsp_ref[...]
        gm_ref[r0:r0 + GM_CHUNK, :] = (u[r0:r0 + GM_CHUNK, :] * s).astype(BF16)

    z_ref[...] = proj(P_Z, SSM_INNER)
    xbc_ref[...] = proj(P_XBC, XBC_WIDTH)
    dt_ref[...] = proj(P_DT, 2 * LANE)

    ca = ca_ref[...]
    cb = cb_ref[...]
    scale = np.float32((QK_NOPE + QK_ROPE) ** -0.5 * math.log2(math.e))

    def rope(t):
        return t * ca + pltpu.roll(t, QK_ROPE, 1) * cb

    cqn = _rms_norm(proj(P_Q, Q_LORA), gq_ref[...]).astype(BF16)
    yq = jnp.dot(cqn, wq_ref[...], preferred_element_type=F32)
    for hh in range(MLA_HEADS):
        c0 = hh * HEAD_PAD
        q_ref[:, c0:c0 + LANE] = (yq[:, c0:c0 + LANE] * scale).astype(BF16)
        q_ref[:, c0 + LANE:c0 + HEAD_PAD] = (rope(yq[:, c0 + LANE:c0 + HEAD_PAD]) * scale).astype(BF16)

    ckvn = _rms_norm(proj(P_KV, KV_LORA), gkv_ref[...]).astype(BF16)
    krf = rope(proj(P_KR, LANE)).astype(BF16)
    yk = jnp.dot(ckvn, wk_ref[...], preferred_element_type=F32)
    for hh in range(MLA_HEADS):
        c0 = hh * HEAD_PAD
        k_ref[:, c0:c0 + LANE] = yk[:, hh * LANE:(hh + 1) * LANE].astype(BF16)
        k_ref[:, c0 + LANE:c0 + HEAD_PAD] = krf
    vt_ref[...] = lax.dot_general(wvt_ref[...], ckvn, (((1,), (1,)), ((), ())),
                                  preferred_element_type=F32).astype(BF16)


def _proj_in(x_all, modv, ca, cb, wp, dims):
    rows = x_all.shape[0]
    tm = ROW_TILE
    nt = rows // tm
    lat_tiles = dims["seq"] // tm
    ctx_tiles = dims["ctx"] // tm
    kv_tiles = lat_tiles + ctx_tiles
    nb = dims["batch"]

    def mod_map(i):
        return (jnp.minimum(i // lat_tiles, nb), 0, 0)

    def kv_pos(i):
        lat_b = i // lat_tiles
        lat_pos = lat_b * kv_tiles + ctx_tiles + i % lat_tiles
        j = i - nb * lat_tiles
        ctx_pos = (j // ctx_tiles) * kv_tiles + j % ctx_tiles
        return jnp.where(i < nb * lat_tiles, lat_pos, ctx_pos)

    row = lambda w: pl.BlockSpec((tm, w), lambda i: (i, 0))
    kv_rows = nb * (dims["seq"] + dims["ctx"])
    return pl.pallas_call(
        _proj_in_kernel,
        grid=(nt,),
        in_specs=[
            row(D_MODEL),
            pl.BlockSpec((1, 6, D_MODEL), mod_map),
            row(LANE), row(LANE),
            _resident((D_MODEL, P_WIDTH)),
            _resident((1, GM_WIDTH)), _resident((1, GM_WIDTH)),
            _resident((GM_HEADS, GM_CHUNK, GM_CHUNK)), _resident((GM_CHUNK, GM_WIDTH)),
            _resident((1, Q_LORA)), _resident((Q_LORA, MLA_HEADS * HEAD_PAD)),
            _resident((1, KV_LORA)), _resident((KV_LORA, MLA_HEADS * QK_NOPE)),
            _resident((MLA_HEADS * V_DIM, KV_LORA)),
        ],
        out_specs=[
            row(GM_WIDTH), row(SSM_INNER), row(XBC_WIDTH), row(2 * LANE),
            row(MLA_HEADS * HEAD_PAD),
            pl.BlockSpec((tm, MLA_HEADS * HEAD_PAD), lambda i: (kv_pos(i), 0)),
            pl.BlockSpec((MLA_HEADS * V_DIM, tm), lambda i: (0, kv_pos(i))),
        ],
        out_shape=[
            jax.ShapeDtypeStruct((rows, GM_WIDTH), BF16),
            jax.ShapeDtypeStruct((rows, SSM_INNER), F32),
            jax.ShapeDtypeStruct((rows, XBC_WIDTH), F32),
            jax.ShapeDtypeStruct((rows, 2 * LANE), F32),
            jax.ShapeDtypeStruct((rows, MLA_HEADS * HEAD_PAD), BF16),
            jax.ShapeDtypeStruct((kv_rows, MLA_HEADS * HEAD_PAD), BF16),
            jax.ShapeDtypeStruct((MLA_HEADS * V_DIM, kv_rows), BF16),
        ],
        compiler_params=_cparams(("arbitrary",)),
        name="proj_in",
    )(x_all, modv, ca, cb, wp["w_in"], wp["g_v"], wp["b_v"], wp["w_sp"], wp["b_sp"],
      wp["g_q"], wp["w_q"], wp["g_kv"], wp["w_k"], wp["w_vt"])


def _ssd_kernel(backward, n_ctx, n_lat, *refs):
    if backward:
        (xp_ref, xc_ref, xn_ref, dt_ref, cw_ref, cbias_ref, dtb_ref, alog_ref, e_ref,
         yf_ref, z_ref, skip_ref, gssm_ref, out_ref, h_scr) = refs
    else:
        (xp_ref, xc_ref, xn_ref, dt_ref, cw_ref, cbias_ref, dtb_ref, alog_ref, e_ref,
         out_ref, h_scr) = refs
    q = SSM_CHUNK
    rows = SSD_STEP * q
    s = pl.program_id(1)
    n_ctx_s = n_ctx // SSD_STEP
    n_lat_s = n_lat // SSD_STEP
    is_ctx = s < n_ctx_s
    if backward:
        cc = jnp.where(is_ctx, n_ctx_s - 1 - s, n_lat_s - 1 - (s - n_ctx_s))
    else:
        cc = jnp.where(is_ctx, s, s - n_ctx_s)
    ncc = jnp.where(is_ctx, n_ctx_s, n_lat_s)
    at_start = cc == 0
    at_end = cc == ncc - 1

    @pl.when(s == 0)
    def _():
        h_scr[...] = jnp.zeros_like(h_scr)

    prev = jnp.where(at_start, 0.0, xp_ref[...])
    nxt = jnp.where(at_end, 0.0, xn_ref[...])
    ext = jnp.concatenate([prev, xc_ref[...], nxt], axis=0)
    acc = jnp.broadcast_to(cbias_ref[...], (rows, XBC_WIDTH))
    base = 8 - SSM_CONV // 2
    for j in range(SSM_CONV):
        acc = acc + cw_ref[j:j + 1, :] * ext[base + j:base + j + rows, :]
    u_all = _silu(acc)

    lane = lax.broadcasted_iota(jnp.int32, (rows, LANE), 1)
    dtr = dt_ref[...] + dtb_ref[...]
    dtv_all = jnp.maximum(dtr, 0.0) + jnp.log1p(jnp.exp(-jnp.abs(dtr)))
    dtv_all = jnp.where(lane < SSM_HEADS, dtv_all, 0.0)
    da_all = dtv_all * (-jnp.exp(alog_ref[...]))

    ri = lax.broadcasted_iota(jnp.int32, (q, q), 0)
    ci = lax.broadcasted_iota(jnp.int32, (q, q), 1)
    mask = (ci >= ri) if backward else (ci <= ri)
    tri = jnp.where(mask, 1.0, 0.0).astype(BF16)
    lane_w = lax.broadcasted_iota(jnp.int32, (q, SSM_INNER), 1) // SSM_HEAD_DIM
    last = 0 if backward else q - 1
    per_g = SSM_HEADS // SSM_GROUPS
    gw = per_g * SSM_HEAD_DIM

    h_cur = h_scr[...]
    for sub in (range(SSD_STEP - 1, -1, -1) if backward else range(SSD_STEP)):
        r0 = sub * q
        u = u_all[r0:r0 + q, :]
        xc = u[:, :SSM_INNER]
        dtv = dtv_all[r0:r0 + q, :]
        acs = _dot_01(tri, da_all[r0:r0 + q, :])
        acs_t = acs.T
        acs_last = acs[last:last + 1, :]
        small = jnp.concatenate([dtv, jnp.exp(acs_last - acs), jnp.exp(acs)], axis=0)
        wide = _dot_01(small, e_ref[...], ones_on_right=True)
        dt_w = wide[:q]
        decay_w = wide[q:2 * q]
        eacs_w = wide[2 * q:]
        elast_w = eacs_w[last:last + 1, :]

        xdt = xc * dt_w
        xw_b = (xdt * decay_w).astype(BF16)
        h_b = h_cur.astype(BF16)

        w_parts = []
        x_parts = []
        inter = []
        upd = []
        for g in range(SSM_GROUPS):
            b_g = u[:, SSM_INNER + g * SSM_STATE:SSM_INNER + (g + 1) * SSM_STATE].astype(BF16)
            o_c = SSM_INNER + SSM_GROUPS * SSM_STATE
            c_g = u[:, o_c + g * SSM_STATE:o_c + (g + 1) * SSM_STATE].astype(BF16)
            cbm = lax.dot_general(c_g, b_g, (((1,), (1,)), ((), ())), preferred_element_type=F32)
            for hh in range(per_g):
                hd = g * per_g + hh
                seg = acs[:, hd:hd + 1] - acs_t[hd:hd + 1, :]
                lmat = jnp.exp(jnp.where(mask, seg, -jnp.inf))
                w_parts.append((cbm * lmat).astype(BF16))
                x_parts.append(jnp.where(lane_w == hd, xdt, 0.0).astype(BF16))
            inter.append(jnp.dot(c_g, h_b[:, g * gw:(g + 1) * gw], preferred_element_type=F32))
            upd.append(lax.dot_general(b_g, xw_b[:, g * gw:(g + 1) * gw], (((0,), (0,)), ((), ())),
                                       preferred_element_type=F32))
        y = jnp.dot(jnp.concatenate(w_parts, axis=1), jnp.concatenate(x_parts, axis=0),
                    preferred_element_type=F32)
        y = y + jnp.concatenate(inter, axis=1) * eacs_w
        h_cur = h_cur * elast_w + jnp.concatenate(upd, axis=1)

        if backward:
            tot = yf_ref[r0:r0 + q, :] + y + skip_ref[...] * xc
            yz = tot * _silu(z_ref[r0:r0 + q, :])
            half = SSM_INNER // SSM_GROUPS
            outs = []
            for g in range(SSM_GROUPS):
                grp = yz[:, g * half:(g + 1) * half]
                outs.append(grp * lax.rsqrt(jnp.mean(grp * grp, axis=-1, keepdims=True) + RMS_EPS))
            out_ref[r0:r0 + q, :] = (jnp.concatenate(outs, axis=1) * gssm_ref[...]).astype(BF16)
        else:
            out_ref[r0:r0 + q, :] = y
    h_scr[...] = h_cur


def _ssd_sweep(backward, xbc, dtc, wp, dims, yf=None, z=None):
    nb = dims["batch"]
    n_lat = dims["seq"] // SSM_CHUNK
    n_ctx = dims["ctx"] // SSM_CHUNK
    assert n_lat % SSD_STEP == 0 and n_ctx % SSD_STEP == 0
    n_lat_s = n_lat // SSD_STEP
    n_ctx_s = n_ctx // SSD_STEP
    ctx_base = nb * n_lat_s
    rows = xbc.shape[0]
    step_rows = SSD_STEP * SSM_CHUNK
    sub = step_rows // 8

    def blk(b, s):
        is_ctx = s < n_ctx_s
        if backward:
            cc = jnp.where(is_ctx, n_ctx_s - 1 - s, n_lat_s - 1 - (s - n_ctx_s))
        else:
            cc = jnp.where(is_ctx, s, s - n_ctx_s)
        return jnp.where(is_ctx, ctx_base + b * n_ctx_s + cc, b * n_lat_s + cc)

    cur = lambda w: pl.BlockSpec((step_rows, w), lambda b, s: (blk(b, s), 0))
    d = 1 if backward else 0
    in_specs = [
        pl.BlockSpec((8, XBC_WIDTH), lambda b, s: (jnp.maximum(blk(b, s) * sub - 1, 0), 0)),
        cur(XBC_WIDTH),
        pl.BlockSpec((8, XBC_WIDTH), lambda b, s: (jnp.minimum((blk(b, s) + 1) * sub, rows // 8 - 1), 0)),
        pl.BlockSpec((step_rows, LANE), lambda b, s: (blk(b, s), d)),
        _resident((8, XBC_WIDTH)), _resident((1, XBC_WIDTH)),
        _resident((1, LANE)), _resident((1, LANE)), _resident((LANE, SSM_INNER)),
    ]
    args = [xbc, xbc, xbc, dtc, wp["conv_w"], wp["conv_b"], wp["dt_bias"][d], wp["a_log"][d], wp["expand"]]
    if backward:
        in_specs += [cur(SSM_INNER), cur(SSM_INNER), _resident((1, SSM_INNER)), _resident((1, SSM_INNER))]
        args += [yf, z, wp["d_skip"], wp["g_ssm"]]
    return pl.pallas_call(
        functools.partial(_ssd_kernel, backward, n_ctx, n_lat),
        grid=(nb, n_ctx_s + n_lat_s),
        in_specs=in_specs,
        out_specs=cur(SSM_INNER),
        out_shape=jax.ShapeDtypeStruct((rows, SSM_INNER), BF16 if backward else F32),
        scratch_shapes=[pltpu.VMEM((SSM_STATE, SSM_INNER), F32)],
        compiler_params=_cparams(("arbitrary", "arbitrary")),
        name="ssd_bwd" if backward else "ssd_fwd",
    )(*args)


def _attn_kernel(nk, tk, q_ref, k_ref, vt_ref, o_ref):
    q = q_ref[...]
    tq = q.shape[0]
    m = jnp.full((1, tq), -jnp.inf, F32)
    l = jnp.zeros((1, tq), F32)
    acc = jnp.zeros((V_DIM, tq), F32)
    for i in range(nk):
        k = k_ref[i * tk:(i + 1) * tk, :]
        vt = vt_ref[:, i * tk:(i + 1) * tk]
        st = lax.dot_general(k, q, (((1,), (1,)), ((), ())), preferred_element_type=F32)
        m_new = jnp.maximum(m, jnp.max(st, axis=0, keepdims=True))
        alpha = jnp.exp2(m - m_new)
        p = jnp.exp2(st - m_new)
        l = alpha * l + jnp.sum(p, axis=0, keepdims=True)
        acc = alpha * acc + jnp.dot(vt, p.astype(BF16), preferred_element_type=F32)
        m = m_new
    o_ref[...] = (acc / l).T.astype(BF16)


def _attention(q, k, vt, nb, lq, lk, q_row0, kv_stride, tq, tk):
    nq = lq // tq
    q0 = q_row0 // tq
    assert kv_stride % lk == 0 and lk % tk == 0 and q_row0 % tq == 0
    kvb = kv_stride // lk
    return pl.pallas_call(
        functools.partial(_attn_kernel, lk // tk, tk),
        grid=(nb, MLA_HEADS, nq),
        in_specs=[
            pl.BlockSpec((tq, HEAD_PAD), lambda b, h, i: (q0 + b * nq + i, h)),
            pl.BlockSpec((lk, HEAD_PAD), lambda b, h, i: (b * kvb, h)),
            pl.BlockSpec((V_DIM, lk), lambda b, h, i: (h, b * kvb)),
        ],
        out_specs=pl.BlockSpec((tq, V_DIM), lambda b, h, i: (b * nq + i, h)),
        out_shape=jax.ShapeDtypeStruct((nb * lq, MLA_HEADS * V_DIM), BF16),
        compiler_params=_cparams(("arbitrary", "arbitrary", "arbitrary")),
        name="mla_attention",
    )(q, k, vt)


def _out_proj_kernel(n_lat_tiles, gm_ref, ssm_ref, att_ref, attc_ref, x_ref, mod_ref, w_ref, g_ref, b_ref, wr_ref,
                     x1_ref, tok_ref, lg_ref):
    mod = mod_ref[0]
    for r0 in range(0, x_ref.shape[0], ROW_TILE):
        rows = slice(r0, r0 + ROW_TILE)
        att = att_ref[rows, :]
        if attc_ref is not None:
            att = jnp.where(pl.program_id(0) >= n_lat_tiles, attc_ref[rows, :], att)
        mixed = jnp.concatenate([gm_ref[rows, :], ssm_ref[rows, :], att], axis=1)
        mix = jnp.dot(mixed, w_ref[...], preferred_element_type=F32)
        x1 = _layer_norm(DN_ALPHA * x_ref[rows, :] + mod[2:3, :] * mix, g_ref[...], b_ref[...])
        x1_ref[rows, :] = x1
        tok = x1 * (1.0 + mod[4:5, :]) + mod[3:4, :]
        tok_ref[rows, :] = _pack_rows(tok)
        lg_ref[:, rows] = lax.dot_general(wr_ref[...], tok, (((1,), (1,)), ((), ())),
                                          precision=HIGHEST, preferred_element_type=F32)


def _out_proj_lat_kernel(n_lat_tiles, gm_ref, ssm_ref, att_ref, *rest):
    _out_proj_kernel(n_lat_tiles, gm_ref, ssm_ref, att_ref, None, *rest)


def _out_proj(gm, ssm, att, att_ctx, x_all, modv, wp, dims, rows):
    tm = OUT_TILE
    assert dims["seq"] % tm == 0 and (dims["batch"] * dims["ctx"]) % tm == 0
    lat_tiles = dims["seq"] // tm
    nb = dims["batch"]
    n_lat = nb * lat_tiles
    row = lambda w: pl.BlockSpec((tm, w), lambda i: (i, 0))
    att_specs = [pl.BlockSpec((tm, MLA_HEADS * V_DIM), lambda i: (jnp.minimum(i, n_lat - 1), 0))]
    att_args = [att]
    body = _out_proj_lat_kernel
    if att_ctx is not None:
        att_specs.append(pl.BlockSpec((tm, MLA_HEADS * V_DIM), lambda i: (jnp.maximum(i - n_lat, 0), 0)))
        att_args.append(att_ctx)
        body = _out_proj_kernel
    return pl.pallas_call(
        functools.partial(body, n_lat),
        grid=(rows // tm,),
        in_specs=[
            row(GM_WIDTH), row(SSM_INNER), *att_specs, row(D_MODEL),
            pl.BlockSpec((1, 6, D_MODEL), lambda i: (jnp.minimum(i // lat_tiles, nb), 0, 0)),
            _resident((D_MODEL, D_MODEL)), _resident((1, D_MODEL)), _resident((1, D_MODEL)),
            _resident((N_EXPERTS, D_MODEL)),
        ],
        out_specs=[row(D_MODEL), row(D_MODEL // 2), pl.BlockSpec((N_EXPERTS, tm), lambda i: (0, i))],
        out_shape=[
            jax.ShapeDtypeStruct((rows, D_MODEL), F32),
            jax.ShapeDtypeStruct((rows, D_MODEL // 2), jnp.uint32),
            jax.ShapeDtypeStruct((N_EXPERTS, rows), F32),
        ],
        compiler_params=_cparams(("arbitrary",)),
        name="out_proj",
    )(gm, ssm, *att_args, x_all, modv, wp["w_out"], wp["ln1_g"], wp["ln1_b"], wp["w_router_t"])


def _gate_kernel(lg_ref, b_ref, eidx_ref, w_ref, pos_ref, cnt_ref, carry_scr):
    per = N_EXPERTS // N_EXPERT_GROUPS
    tn = lg_ref.shape[1]
    sc = jax.nn.sigmoid(lg_ref[...])
    bi = sc + b_ref[...]
    neg = np.float32(-np.inf)
    xs = [bi[j * N_EXPERT_GROUPS:(j + 1) * N_EXPERT_GROUPS, :] for j in range(per)]
    ss = [sc[j * N_EXPERT_GROUPS:(j + 1) * N_EXPERT_GROUPS, :] for j in range(per)]
    m1 = functools.reduce(jnp.maximum, xs)
    first = functools.reduce(jnp.minimum, [jnp.where(xs[j] == m1, j, per) for j in range(per)])
    m2 = functools.reduce(jnp.maximum, [jnp.where(first == j, neg, xs[j]) for j in range(per)])
    gscore = m1 + m2
    gi = lax.broadcasted_iota(jnp.int32, (N_EXPERT_GROUPS, tn), 0)
    cur = gscore
    gsel = jnp.zeros((N_EXPERT_GROUPS, tn), jnp.int32)
    for _ in range(TOPK_GROUPS):
        mx = jnp.max(cur, axis=0, keepdims=True)
        fi = jnp.min(jnp.where(cur == mx, gi, N_EXPERT_GROUPS), axis=0, keepdims=True)
        pick = gi == fi
        gsel = jnp.where(pick, 1, gsel)
        cur = jnp.where(pick, neg, cur)
    cur = [jnp.where(gsel > 0, xs[j], neg) for j in range(per)]
    eid = [gi * per + j for j in range(per)]
    sel = [jnp.zeros((N_EXPERT_GROUPS, tn), F32) for _ in range(per)]
    picked = []
    for _ in range(TOP_K):
        mx = jnp.max(functools.reduce(jnp.maximum, cur), axis=0, keepdims=True)
        cand = functools.reduce(jnp.minimum, [jnp.where(cur[j] == mx, eid[j], N_EXPERTS) for j in range(per)])
        fi = jnp.min(cand, axis=0, keepdims=True)
        wk = jnp.zeros((N_EXPERT_GROUPS, tn), F32)
        for j in range(per):
            pick = eid[j] == fi
            sel[j] = jnp.where(pick, 1.0, sel[j])
            cur[j] = jnp.where(pick, neg, cur[j])
            wk = wk + jnp.where(pick, ss[j], 0.0)
        picked.append((fi, jnp.sum(wk, axis=0, keepdims=True)))
    tot = functools.reduce(jnp.add, [wk for _, wk in picked])

    @pl.when(pl.program_id(0) == 0)
    def _():
        carry_scr[...] = jnp.zeros_like(carry_scr)

    sel_all = jnp.concatenate(sel, axis=0)
    ri = lax.broadcasted_iota(jnp.int32, (tn, tn), 0)
    ci = lax.broadcasted_iota(jnp.int32, (tn, tn), 1)
    before = jnp.where(ri < ci, 1.0, 0.0).astype(BF16)
    prefix = jnp.dot(sel_all.astype(BF16), before, preferred_element_type=F32)
    carry = carry_scr[...]
    rank = prefix + carry
    carry_new = carry + prefix[:, tn - 1:tn] + sel_all[:, tn - 1:tn]
    carry_scr[...] = carry_new
    cnt_ref[...] = jnp.broadcast_to(carry_new, cnt_ref.shape).astype(jnp.int32)
    for k, (fi, wk) in enumerate(picked):
        pk = jnp.zeros((N_EXPERT_GROUPS, tn), F32)
        for j in range(per):
            pk = pk + jnp.where(eid[j] == fi, rank[j * N_EXPERT_GROUPS:(j + 1) * N_EXPERT_GROUPS, :], 0.0)
        eidx_ref[k:k + 1, :] = fi
        w_ref[k:k + 1, :] = wk / tot * ROUTED_SCALE
        pos_ref[k:k + 1, :] = jnp.sum(pk, axis=0, keepdims=True).astype(jnp.int32)


def _gate(logits_t, b_router_col):
    t = logits_t.shape[1]
    tn = 512
    tile = pl.BlockSpec((TOP_K, tn), lambda i: (0, i))
    return pl.pallas_call(
        _gate_kernel,
        grid=(t // tn,),
        in_specs=[pl.BlockSpec((N_EXPERTS, tn), lambda i: (0, i)), _resident((N_EXPERTS, 1))],
        out_specs=[tile, tile, tile, pl.BlockSpec((N_EXPERTS, LANE), lambda i: (0, 0))],
        out_shape=[
            jax.ShapeDtypeStruct((TOP_K, t), jnp.int32),
            jax.ShapeDtypeStruct((TOP_K, t), F32),
            jax.ShapeDtypeStruct((TOP_K, t), jnp.int32),
            jax.ShapeDtypeStruct((N_EXPERTS, LANE), jnp.int32),
        ],
        scratch_shapes=[pltpu.VMEM((N_EXPERTS, 1), F32)],
        compiler_params=_cparams(("arbitrary",)),
        name="gate_topk",
    )(logits_t, b_router_col)


DISPATCH_TILE = 512


def _dispatch_kernel(padlo_ref, padn_ref, nu_ref, slot_ref, tok_ref, xs_hbm, zero_buf, sem, zsem):
    i = pl.program_id(0)
    td = slot_ref.shape[1]

    def row_copy(r, k):
        return pltpu.make_async_copy(tok_ref.at[pl.ds(r, 1)], xs_hbm.at[pl.ds(slot_ref[k, r], 1)], sem)

    def issue(r, carry):
        for k in range(TOP_K):
            row_copy(r, k).start()
        return carry

    lax.fori_loop(0, td, issue, 0)

    @pl.when(i == 0)
    def _():
        zero_buf[...] = jnp.zeros_like(zero_buf)

        def fill(e, carry):
            lo = padlo_ref[e]
            n = padn_ref[e]
            head = jnp.minimum((-lo) & 7, n)

            for s in range(7):
                @pl.when(s < head)
                def _(s=s):
                    cp = pltpu.make_async_copy(zero_buf.at[pl.ds(0, 1)], xs_hbm.at[pl.ds(lo + s, 1)], zsem)
                    cp.start()
                    cp.wait()

            rest = n - head
            off = lo + head
            size = EXPERT_BLOCK // 2
            while size >= 8:
                hit = (rest & size) != 0

                @pl.when(hit)
                def _(off=off, size=size):
                    dst = xs_hbm.at[pl.ds(pl.multiple_of(off, 8), size)]
                    cp = pltpu.make_async_copy(zero_buf.at[pl.ds(0, size)], dst, zsem)
                    cp.start()
                    cp.wait()

                off = off + jnp.where(hit, size, 0)
                size //= 2
            return carry

        lax.fori_loop(0, N_EXPERTS, fill, 0)

        zrows = zero_buf.shape[0]

        def fill_tail(b, carry):
            for h in range(EXPERT_BLOCK // zrows):
                row0 = pl.multiple_of(b * EXPERT_BLOCK + h * zrows, 8)
                cp = pltpu.make_async_copy(zero_buf, xs_hbm.at[pl.ds(row0, zrows)], zsem)
                cp.start()
                cp.wait()
            return carry

        lax.fori_loop(nu_ref[0], xs_hbm.shape[0] // EXPERT_BLOCK, fill_tail, 0)

    for _ in range(TOP_K):
        pltpu.make_async_copy(tok_ref, xs_hbm.at[pl.ds(0, td)], sem).wait()


def _dispatch_rows(slot, pad_lo, pad_n, n_used, tok, cap):
    t = tok.shape[0]
    td = DISPATCH_TILE
    return pl.pallas_call(
        _dispatch_kernel,
        grid_spec=pltpu.PrefetchScalarGridSpec(
            num_scalar_prefetch=3,
            grid=(t // td,),
            in_specs=[
                pl.BlockSpec((TOP_K, td), lambda i, lo, n, nu: (0, i), memory_space=pltpu.SMEM),
                pl.BlockSpec((td, D_MODEL // 2), lambda i, lo, n, nu: (i, 0)),
            ],
            out_specs=pl.BlockSpec(memory_space=pl.ANY),
            scratch_shapes=[pltpu.VMEM((EXPERT_BLOCK // 2, D_MODEL // 2), jnp.uint32),
                            pltpu.SemaphoreType.DMA, pltpu.SemaphoreType.DMA],
        ),
        out_shape=jax.ShapeDtypeStruct((cap, D_MODEL // 2), jnp.uint32),
        compiler_params=_cparams(("arbitrary",)),
        name="dispatch_rows",
    )(pad_lo, pad_n, n_used, slot, tok)


def _experts_kernel(be_ref, nu_ref, x_ref, wg_ref, wu_ref, wd_ref, y_ref, wg_s, wu_s, wd_s):
    j = pl.program_id(0)
    prev = be_ref[jnp.maximum(j - 1, 0)]
    fresh = jnp.logical_or(j == 0, be_ref[j] != prev)

    @pl.when(fresh)
    def _():
        wg_s[...] = wg_ref[0].astype(BF16)
        wu_s[...] = wu_ref[0].astype(BF16)
        wd_s[...] = wd_ref[0].astype(BF16)

    @pl.when(j < nu_ref[0])
    def _():
        x = _unpack_rows(x_ref[...])
        g = jnp.dot(x, wg_s[...], preferred_element_type=F32)
        u = jnp.dot(x, wu_s[...], preferred_element_type=F32)
        hb = (_silu(g) * u).astype(BF16)
        y_ref[...] = _pack_rows(jnp.dot(hb, wd_s[...], preferred_element_type=F32))

    @pl.when(j >= nu_ref[0])
    def _():
        y_ref[...] = jnp.zeros_like(y_ref)


def _experts(layer, block_e, n_used, x_sorted, w_gate, w_up, w_down):
    cap = x_sorted.shape[0]
    blk = EXPERT_BLOCK
    return pl.pallas_call(
        _experts_kernel,
        grid_spec=pltpu.PrefetchScalarGridSpec(
            num_scalar_prefetch=2,
            grid=(cap // blk,),
            in_specs=[
                pl.BlockSpec((blk, D_MODEL // 2), lambda j, be, nu: (jnp.minimum(j, nu[0] - 1), 0)),
                pl.BlockSpec((None, 1, D_MODEL, EXPERT_FF), lambda j, be, nu: (layer, be[j], 0, 0)),
                pl.BlockSpec((None, 1, D_MODEL, EXPERT_FF), lambda j, be, nu: (layer, be[j], 0, 0)),
                pl.BlockSpec((None, 1, EXPERT_FF, D_MODEL), lambda j, be, nu: (layer, be[j], 0, 0)),
            ],
            out_specs=pl.BlockSpec((blk, D_MODEL // 2), lambda j, be, nu: (j, 0)),
            scratch_shapes=[pltpu.VMEM((D_MODEL, EXPERT_FF), BF16), pltpu.VMEM((D_MODEL, EXPERT_FF), BF16),
                            pltpu.VMEM((EXPERT_FF, D_MODEL), BF16)],
        ),
        out_shape=jax.ShapeDtypeStruct((cap, D_MODEL // 2), jnp.uint32),
        compiler_params=_cparams(("arbitrary",)),
        name="routed_experts",
    )(block_e, n_used, x_sorted, w_gate, w_up, w_down)


COMBINE_TILE = 256


def _final_kernel(slot_ref, slot_next_ref, w8_ref, tok_ref, x1_ref, mod_ref, wgu_ref, wd_ref, g_ref, b_ref,
                  y_hbm, o_ref, ybuf, sems):
    i = pl.program_id(0)
    n = pl.num_programs(0)
    tf = tok_ref.shape[0]
    cur = i % 2

    def gather(slots, buf):
        def issue(r, carry):
            for k in range(TOP_K):
                pltpu.make_async_copy(y_hbm.at[pl.ds(slots[k, r], 1)], ybuf.at[buf, k, pl.ds(r, 1)],
                                      sems.at[buf]).start()
            return carry

        lax.fori_loop(0, tf, issue, 0)

    @pl.when(i == 0)
    def _():
        gather(slot_ref, 0)

    @pl.when(i + 1 < n)
    def _():
        gather(slot_next_ref, 1 - cur)

    mod = mod_ref[0]
    gu = jnp.dot(_unpack_rows(tok_ref[...]), wgu_ref[...], preferred_element_type=F32)
    hb = (_silu(gu[:, :SHARED_FF]) * gu[:, SHARED_FF:]).astype(BF16)
    f = jnp.dot(hb, wd_ref[...], preferred_element_type=F32)

    for k in range(TOP_K):
        pltpu.make_async_copy(y_hbm.at[pl.ds(0, tf)], ybuf.at[cur, k], sems.at[cur]).wait()
    w8 = w8_ref[...]
    half = D_MODEL // 2
    f_lo = f[:, :half]
    f_hi = f[:, half:]
    for k in range(TOP_K):
        yk = ybuf[cur, k]
        wk = w8[:, k:k + 1]
        f_lo = f_lo + wk * pltpu.bitcast(yk << 16, F32)
        f_hi = f_hi + wk * pltpu.bitcast(yk & jnp.uint32(0xFFFF0000), F32)
    f = jnp.concatenate([f_lo, f_hi], axis=1)
    o_ref[...] = _layer_norm(DN_ALPHA * x1_ref[...] + mod[5:6, :] * f, g_ref[...], b_ref[...])


def _final(slot, w8_t, tok, y_sorted, x1, modv, wp, dims, rows):
    tf = COMBINE_TILE
    nt = rows // tf
    lat_tiles = dims["seq"] // tf
    nb = dims["batch"]
    row = lambda w: pl.BlockSpec((tf, w), lambda i: (i, 0))
    return pl.pallas_call(
        _final_kernel,
        grid=(nt,),
        in_specs=[
            pl.BlockSpec((TOP_K, tf), lambda i: (0, i), memory_space=pltpu.SMEM),
            pl.BlockSpec((TOP_K, tf), lambda i: (0, jnp.minimum(i + 1, nt - 1)), memory_space=pltpu.SMEM),
            row(TOP_K), row(D_MODEL // 2), row(D_MODEL),
            pl.BlockSpec((1, 6, D_MODEL), lambda i: (jnp.minimum(i // lat_tiles, nb), 0, 0)),
            _resident((D_MODEL, 2 * SHARED_FF)), _resident((SHARED_FF, D_MODEL)),
            _resident((1, D_MODEL)), _resident((1, D_MODEL)),
            pl.BlockSpec(memory_space=pl.ANY),
        ],
        out_specs=row(D_MODEL),
        out_shape=jax.ShapeDtypeStruct((rows, D_MODEL), F32),
        scratch_shapes=[pltpu.VMEM((2, TOP_K, tf, D_MODEL // 2), jnp.uint32), pltpu.SemaphoreType.DMA((2,))],
        compiler_params=_cparams(("arbitrary",)),
        name="combine_shared_ln2",
    )(slot, slot, w8_t, tok, x1, modv, wp["w_sh_gu"], wp["w_sh_down"], wp["ln2_g"], wp["ln2_b"], y_sorted)


def _rot_half_cols():
    src = np.zeros((QK_ROPE,), np.int32)
    sign = np.zeros((QK_ROPE,), np.float32)
    quarter = QK_ROPE // 4
    for dcol in range(QK_ROPE):
        part, i = divmod(dcol, 2 * quarter)
        half, kk = divmod(i, quarter)
        src[dcol] = part * 2 * quarter + (quarter + kk if half == 0 else kk)
        sign[dcol] = -1.0 if half == 0 else 1.0
    return src, sign


def _expert_major(a, axis):
    per = N_EXPERTS // N_EXPERT_GROUPS
    shp = a.shape
    a = a.reshape(shp[:axis] + (N_EXPERT_GROUPS, per) + shp[axis + 1:])
    a = jnp.swapaxes(a, axis, axis + 1)
    return a.reshape(shp)


def _prep_layer(l, p):
    src, sign = _rot_half_cols()
    w_in = p["w_in"][l]
    col = lambda lo, n: w_in[:, lo:lo + n]
    zpad = jnp.zeros((D_MODEL, LANE - SSM_HEADS), F32)
    w_kr = col(R_KR, QK_ROPE)
    w_in_p = jnp.concatenate([
        col(R_GM, 2 * GM_WIDTH), col(R_Q, Q_LORA), col(R_Z, SSM_INNER), col(R_XBC, XBC_WIDTH),
        col(R_DT, SSM_HEADS), zpad, col(R_DT + SSM_HEADS, SSM_HEADS), zpad,
        col(R_KV, KV_LORA), w_kr, w_kr[:, src] * sign,
    ], axis=1).astype(BF16)
    wq = p["w_q_b"][l].reshape(Q_LORA, MLA_HEADS, QK_NOPE + QK_ROPE)
    wq_rope = wq[:, :, QK_NOPE:]
    w_q = jnp.concatenate([wq[:, :, :QK_NOPE], wq_rope, wq_rope[:, :, src] * sign], axis=2)
    w_q = w_q.reshape(Q_LORA, MLA_HEADS * HEAD_PAD).astype(BF16)
    wkv = p["w_kv_b"][l].reshape(KV_LORA, MLA_HEADS, QK_NOPE + V_DIM)
    w_k = wkv[:, :, :QK_NOPE].reshape(KV_LORA, -1).astype(BF16)
    w_vt = wkv[:, :, QK_NOPE:].reshape(KV_LORA, -1).T.astype(BF16)
    lane_pad = lambda a: jnp.pad(a, ((0, 0), (0, LANE - a.shape[1])))
    expand = jnp.repeat(jnp.eye(LANE, SSM_HEADS, dtype=BF16), SSM_HEAD_DIM, axis=1)
    r2 = lambda a: a.reshape(1, -1)
    return dict(
        w_in=w_in_p, g_v=r2(p["g_v"][l]), b_v=r2(p["b_v"][l]), w_sp=p["w_sp"][l].astype(BF16),
        b_sp=jnp.repeat(p["b_sp"][l].T, GM_CHUNK, axis=1),
        g_q=r2(p["g_q"][l]), w_q=w_q, g_kv=r2(p["g_kv"][l]), w_k=w_k, w_vt=w_vt,
        conv_w=jnp.pad(p["conv_w"][l], ((0, 8 - SSM_CONV), (0, 0))), conv_b=r2(p["conv_b"][l]),
        dt_bias=lane_pad(p["dt_bias"][l])[:, None, :], a_log=lane_pad(p["a_log"][l])[:, None, :],
        expand=expand, d_skip=r2(jnp.repeat(p["d_skip"][l], SSM_HEAD_DIM)), g_ssm=r2(p["g_ssm"][l]),
        w_out=p["w_out"][l].astype(BF16), ln1_g=r2(p["ln1_g"][l]), ln1_b=r2(p["ln1_b"][l]),
        ln2_g=r2(p["ln2_g"][l]), ln2_b=r2(p["ln2_b"][l]),
        w_router_t=_expert_major(p["w_router"][l].T, 0),
        b_router=_expert_major(p["b_router"][l], 0).reshape(N_EXPERTS, 1),
        w_sh_gu=jnp.concatenate([p["w_sh_gate"][l], p["w_sh_up"][l]], axis=1).astype(BF16),
        w_sh_down=p["w_sh_down"][l].astype(BF16),
    )


def _rope_tables(nb, seq, nctx):
    t = jnp.arange(seq)
    half = QK_ROPE // 2
    inv = 1.0 / (ROPE_BASE ** (jnp.arange(0, half, 2, dtype=F32) / half))
    ang_r = (t // GRID_W)[:, None] * inv
    ang_c = (t % GRID_W)[:, None] * inv
    ang = jnp.concatenate([ang_r, ang_r, ang_c, ang_c], axis=1)
    zero = jnp.zeros((seq, LANE - QK_ROPE), F32)
    ca_lat = jnp.concatenate([jnp.cos(ang), zero], axis=1)
    cb_lat = jnp.concatenate([jnp.sin(ang), zero], axis=1)
    ca_ctx = jnp.concatenate([jnp.ones((nb * nctx, QK_ROPE), F32), jnp.zeros((nb * nctx, LANE - QK_ROPE), F32)], axis=1)
    ca = jnp.concatenate([jnp.tile(ca_lat, (nb, 1)), ca_ctx], axis=0)
    cb = jnp.concatenate([jnp.tile(cb_lat, (nb, 1)), jnp.zeros((nb * nctx, LANE), F32)], axis=0)
    return ca, cb


def _slot_layout(eidx, pos, cnt, t):
    per = N_EXPERTS // N_EXPERT_GROUPS
    blk = EXPERT_BLOCK
    counts = cnt[:, 0].reshape(per, N_EXPERT_GROUPS).T.reshape(N_EXPERTS)
    padded = (counts + blk - 1) // blk * blk
    pend = jnp.cumsum(padded)
    pstart = pend - padded
    n_blocks = (t * TOP_K + N_EXPERTS * (blk - 1) + blk - 1) // blk
    block_e = jnp.sum((pend[None, :] <= (jnp.arange(n_blocks) * blk)[:, None]).astype(jnp.int32), axis=1)
    block_e = jnp.minimum(block_e, N_EXPERTS - 1)
    n_used = (pend[-1] // blk).reshape(1)
    hit = eidx[:, :, None] == jnp.arange(N_EXPERTS, dtype=jnp.int32)
    slot = (jnp.sum(jnp.where(hit, pstart, 0), axis=2) + pos).astype(jnp.int32)
    return (slot, block_e.astype(jnp.int32), n_used.astype(jnp.int32), (pstart + counts).astype(jnp.int32),
            (padded - counts).astype(jnp.int32), n_blocks * blk)


def kernel(x, c, ctx, c_ctx, w_mod, b_mod, w_in, g_q, w_q_b, g_kv, w_kv_b, conv_w, conv_b, a_log, dt_bias, d_skip, g_ssm, g_v, b_v, w_sp, b_sp, w_out, ln1_g, ln1_b, ln2_g, ln2_b, w_router, b_router, w_e_gate, w_e_up, w_e_down, w_sh_gate, w_sh_up, w_sh_down):
    p = dict(w_in=w_in, g_q=g_q, w_q_b=w_q_b, g_kv=g_kv, w_kv_b=w_kv_b, conv_w=conv_w, conv_b=conv_b,
             a_log=a_log, dt_bias=dt_bias, d_skip=d_skip, g_ssm=g_ssm, g_v=g_v, b_v=b_v, w_sp=w_sp, b_sp=b_sp,
             w_out=w_out, ln1_g=ln1_g, ln1_b=ln1_b, ln2_g=ln2_g, ln2_b=ln2_b, w_router=w_router,
             b_router=b_router, w_sh_gate=w_sh_gate, w_sh_up=w_sh_up, w_sh_down=w_sh_down)
    nb, seq, d = x.shape
    nctx = ctx.shape[1]
    depth = w_mod.shape[0]
    assert d == D_MODEL and depth == DEPTH and nb + 1 <= 8
    assert seq % ROW_TILE == 0 and nctx % ROW_TILE == 0 and seq % GRID_W == 0
    dims = dict(batch=nb, seq=seq, ctx=nctx)
    lat_rows = nb * seq
    rows_all = lat_rows + nb * nctx
    lk = seq + nctx

    x_all = jnp.concatenate([x.reshape(lat_rows, d), ctx.reshape(nb * nctx, d)], axis=0)
    cvec = jnp.zeros((8, d), F32).at[:nb].set(c).at[nb].set(c_ctx)
    mods = _modulation(cvec, w_mod, b_mod).reshape(depth, 8, 6, d)[:, :nb + 1]
    ca, cb = _rope_tables(nb, seq, nctx)
    tq = min(1024, seq)
    tk = 768 if lk % 768 == 0 else ROW_TILE

    for l in range(depth):
        last = l == depth - 1
        wp = _prep_layer(l, p)
        modv = mods[l]
        gm, z, xbc, dtc, q, k, vt = _proj_in(x_all, modv, ca, cb, wp, dims)
        yf = _ssd_sweep(False, xbc, dtc, wp, dims)
        ssm = _ssd_sweep(True, xbc, dtc, wp, dims, yf=yf, z=z)
        att = _attention(q, k, vt, nb, seq, lk, 0, lk, tq, tk)
        rows = lat_rows if last else rows_all
        att_ctx = None if last else _attention(q, k, vt, nb, nctx, nctx, lat_rows, lk, ROW_TILE, ROW_TILE)
        x1, tok, logits_t = _out_proj(gm, ssm, att, att_ctx, x_all, modv, wp, dims, rows)
        eidx, w8, pos, cnt = _gate(logits_t, wp["b_router"])
        slot, block_e, n_used, pad_lo, pad_n, cap = _slot_layout(eidx, pos, cnt, rows)
        x_sorted = _dispatch_rows(slot, pad_lo, pad_n, n_used, tok, cap)
        y_sorted = _experts(l, block_e, n_used, x_sorted, w_e_gate, w_e_up, w_e_down)
        x_all = _final(slot, w8.T, tok, y_sorted, x1, modv, wp, dims, rows)
    return x_all.reshape(nb, seq, d)
```

```python
import functools
import math

import jax
import jax.numpy as jnp
import numpy as np
from jax import lax
from jax.experimental import pallas as pl
from jax.experimental.pallas import tpu as pltpu

F32 = jnp.float32
BF16 = jnp.bfloat16
HIGHEST = lax.Precision.HIGHEST

D_MODEL = 2048
GRID_W = 64
GM_HEADS = 4
GM_WIDTH = 512
GM_CHUNK = 128
SSM_HEADS = 8
SSM_HEAD_DIM = 64
SSM_INNER = 512
SSM_GROUPS = 2
SSM_STATE = 128
SSM_CONV = 5
SSM_CHUNK = 128
XBC_WIDTH = 1024
MLA_HEADS = 8
QK_NOPE = 128
QK_ROPE = 64
V_DIM = 128
Q_LORA = 768
KV_LORA = 256
ROPE_BASE = 10000.0
N_EXPERTS = 64
TOP_K = 8
N_EXPERT_GROUPS = 8
TOPK_GROUPS = 4
EXPERT_FF = 512
SHARED_FF = 512
ROUTED_SCALE = 2.5
LN_EPS = 1e-5
RMS_EPS = 1e-6
DEPTH = 2
DN_ALPHA = (2 * DEPTH) ** 0.25

R_GM = 0
R_Q = R_GM + 2 * GM_WIDTH
R_Z = R_Q + Q_LORA
R_XBC = R_Z + SSM_INNER
R_DT = R_XBC + XBC_WIDTH
R_KV = R_DT + 2 * SSM_HEADS
R_KR = R_KV + KV_LORA

LANE = 128
P_GM = 0
P_Q = P_GM + 2 * GM_WIDTH
P_Z = P_Q + Q_LORA
P_XBC = P_Z + SSM_INNER
P_DT = P_XBC + XBC_WIDTH
P_KV = P_DT + 2 * LANE
P_KR = P_KV + KV_LORA
P_WIDTH = P_KR + LANE

HEAD_PAD = 256
ROW_TILE = 256
OUT_TILE = 2 * ROW_TILE
SSD_STEP = 2
EXPERT_BLOCK = 512
VMEM_LIMIT = 56 * 1024 * 1024


def _cparams(sem):
    return pltpu.CompilerParams(dimension_semantics=sem, vmem_limit_bytes=VMEM_LIMIT)


def _resident(shape):
    n = len(shape)
    return pl.BlockSpec(shape, lambda *_: (0,) * n, pipeline_mode=pl.Buffered(1))


def _silu(x):
    return x * jax.nn.sigmoid(x)


def _layer_norm(y, g, b):
    mu = jnp.mean(y, axis=-1, keepdims=True)
    yc = y - mu
    var = jnp.mean(yc * yc, axis=-1, keepdims=True)
    return yc * lax.rsqrt(var + LN_EPS) * g + b


def _rms_norm(y, g):
    return y * lax.rsqrt(jnp.mean(y * y, axis=-1, keepdims=True) + RMS_EPS) * g


def _dot_01(a, b, ones_on_right=False):
    val, sel = (a, b) if ones_on_right else (b, a)
    out = None
    for _ in range(3):
        piece = val.astype(BF16)
        term = (jnp.dot(piece, sel, preferred_element_type=F32) if ones_on_right
                else jnp.dot(sel, piece, preferred_element_type=F32))
        out = term if out is None else out + term
        val = val - piece.astype(F32)
    return out


def _pack_rows(t):
    n = t.shape[1] // 2
    bits = pltpu.bitcast(t.astype(BF16).astype(F32), jnp.uint32)
    return (bits[:, :n] >> 16) | bits[:, n:]


def _unpack_rows(w):
    lo = pltpu.bitcast(w << 16, F32).astype(BF16)
    hi = pltpu.bitcast(w & jnp.uint32(0xFFFF0000), F32).astype(BF16)
    return jnp.concatenate([lo, hi], axis=1)


def _mod_kernel(c_ref, w_ref, b_ref, o_ref):
    c = c_ref[...]
    o_ref[0] = jnp.dot(_silu(c), w_ref[0], precision=HIGHEST, preferred_element_type=F32) + b_ref[0]


def _modulation(cvec, w_mod, b_mod):
    depth, d, n = w_mod.shape
    tn = 1536
    return pl.pallas_call(
        _mod_kernel,
        grid=(depth, n // tn),
        in_specs=[
            pl.BlockSpec((8, d), lambda l, j: (0, 0)),
            pl.BlockSpec((1, d, tn), lambda l, j: (l, 0, j)),
            pl.BlockSpec((1, 1, tn), lambda l, j: (l, 0, j)),
        ],
        out_specs=pl.BlockSpec((1, 8, tn), lambda l, j: (l, 0, j)),
        out_shape=jax.ShapeDtypeStruct((depth, 8, n), F32),
        compiler_params=_cparams(("arbitrary", "arbitrary")),
        name="modulation",
    )(cvec, w_mod, b_mod.reshape(depth, 1, n))


def _proj_in_kernel(x_ref, mod_ref, ca_ref, cb_ref, w_ref, gv_ref, bv_ref, wsp_ref, bsp_ref,
                    gq_ref, wq_ref, gkv_ref, wk_ref, wvt_ref,
                    gm_ref, z_ref, xbc_ref, dt_ref, q_ref, k_ref, vt_ref):
    tm = x_ref.shape[0]
    mod = mod_ref[0]
    h = (x_ref[...] * (1.0 + mod[1:2, :]) + mod[0:1, :]).astype(BF16)

    def proj(lo, n):
        return jnp.dot(h, w_ref[:, lo:lo + n], preferred_element_type=F32)

    uv = proj(P_GM, 2 * GM_WIDTH)
    gl = 0.5 * uv * (1.0 + lax.erf(uv * np.float32(math.sqrt(0.5))))
    u = gl[:, :GM_WIDTH]
    vn = _layer_norm(gl[:, GM_WIDTH:], gv_ref[...], bv_ref[...]).astype(BF16)
    for c in range(tm // GM_CHUNK):
        r0 = c * GM_CHUNK
        parts = [jnp.dot(wsp_ref[hh], vn[r0:r0 + GM_CHUNK, hh * LANE:(hh + 1) * LANE],
                         preferred_element_type=F32) for hh in range(GM_HEADS)]
        s = jnp.concatenate(parts, axis=1) + bsp_ref[...]
        gm_ref[r0:r0 + GM_CHUNK, :] = (u[r0:r0 + GM_CHUNK, :] * s).astype(BF16)

    z_ref[...] = proj(P_Z, SSM_INNER)
    xbc_ref[...] = proj(P_XBC, XBC_WIDTH)
    dt_ref[...] = proj(P_DT, 2 * LANE)

    ca = ca_ref[...]
    cb = cb_ref[...]
    scale = np.float32((QK_NOPE + QK_ROPE) ** -0.5 * math.log2(math.e))

    def rope(t):
        return t * ca + pltpu.roll(t, QK_ROPE, 1) * cb

    cqn = _rms_norm(proj(P_Q, Q_LORA), gq_ref[...]).astype(BF16)
    yq = jnp.dot(cqn, wq_ref[...], preferred_element_type=F32)
    for hh in range(MLA_HEADS):
        c0 = hh * HEAD_PAD
        q_ref[:, c0:c0 + LANE] = (yq[:, c0:c0 + LANE] * scale).astype(BF16)
        q_ref[:, c0 + LANE:c0 + HEAD_PAD] = (rope(yq[:, c0 + LANE:c0 + HEAD_PAD]) * scale).astype(BF16)

    ckvn = _rms_norm(proj(P_KV, KV_LORA), gkv_ref[...]).astype(BF16)
    krf = rope(proj(P_KR, LANE)).astype(BF16)
    yk = jnp.dot(ckvn, wk_ref[...], preferred_element_type=F32)
    for hh in range(MLA_HEADS):
        c0 = hh * HEAD_PAD
        k_ref[:, c0:c0 + LANE] = yk[:, hh * LANE:(hh + 1) * LANE].astype(BF16)
        k_ref[:, c0 + LANE:c0 + HEAD_PAD] = krf
    vt_ref[...] = lax.dot_general(wvt_ref[...], ckvn, (((1,), (1,)), ((), ())),
                                  preferred_element_type=F32).astype(BF16)


def _proj_in(x_all, modv, ca, cb, wp, dims):
    rows = x_all.shape[0]
    tm = ROW_TILE
    nt = rows // tm
    lat_tiles = dims["seq"] // tm
    ctx_tiles = dims["ctx"] // tm
    kv_tiles = lat_tiles + ctx_tiles
    nb = dims["batch"]

    def mod_map(i):
        return (jnp.minimum(i // lat_tiles, nb), 0, 0)

    def kv_pos(i):
        lat_b = i // lat_tiles
        lat_pos = lat_b * kv_tiles + ctx_tiles + i % lat_tiles
        j = i - nb * lat_tiles
        ctx_pos = (j // ctx_tiles) * kv_tiles + j % ctx_tiles
        return jnp.where(i < nb * lat_tiles, lat_pos, ctx_pos)

    row = lambda w: pl.BlockSpec((tm, w), lambda i: (i, 0))
    kv_rows = nb * (dims["seq"] + dims["ctx"])
    return pl.pallas_call(
        _proj_in_kernel,
        grid=(nt,),
        in_specs=[
            row(D_MODEL),
            pl.BlockSpec((1, 6, D_MODEL), mod_map),
            row(LANE), row(LANE),
            _resident((D_MODEL, P_WIDTH)),
            _resident((1, GM_WIDTH)), _resident((1, GM_WIDTH)),
            _resident((GM_HEADS, GM_CHUNK, GM_CHUNK)), _resident((GM_CHUNK, GM_WIDTH)),
            _resident((1, Q_LORA)), _resident((Q_LORA, MLA_HEADS * HEAD_PAD)),
            _resident((1, KV_LORA)), _resident((KV_LORA, MLA_HEADS * QK_NOPE)),
            _resident((MLA_HEADS * V_DIM, KV_LORA)),
        ],
        out_specs=[
            row(GM_WIDTH), row(SSM_INNER), row(XBC_WIDTH), row(2 * LANE),
            row(MLA_HEADS * HEAD_PAD),
            pl.BlockSpec((tm, MLA_HEADS * HEAD_PAD), lambda i: (kv_pos(i), 0)),
            pl.BlockSpec((MLA_HEADS * V_DIM, tm), lambda i: (0, kv_pos(i))),
        ],
        out_shape=[
            jax.ShapeDtypeStruct((rows, GM_WIDTH), BF16),
            jax.ShapeDtypeStruct((rows, SSM_INNER), F32),
            jax.ShapeDtypeStruct((rows, XBC_WIDTH), F32),
            jax.ShapeDtypeStruct((rows, 2 * LANE), F32),
            jax.ShapeDtypeStruct((rows, MLA_HEADS * HEAD_PAD), BF16),
            jax.ShapeDtypeStruct((kv_rows, MLA_HEADS * HEAD_PAD), BF16),
            jax.ShapeDtypeStruct((MLA_HEADS * V_DIM, kv_rows), BF16),
        ],
        compiler_params=_cparams(("arbitrary",)),
        name="proj_in",
    )(x_all, modv, ca, cb, wp["w_in"], wp["g_v"], wp["b_v"], wp["w_sp"], wp["b_sp"],
      wp["g_q"], wp["w_q"], wp["g_kv"], wp["w_k"], wp["w_vt"])


def _ssd_kernel(backward, n_ctx, n_lat, *refs):
    if backward:
        (xp_ref, xc_ref, xn_ref, dt_ref, cw_ref, cbias_ref, dtb_ref, alog_ref, e_ref,
         yf_ref, z_ref, skip_ref, gssm_ref, out_ref, h_scr) = refs
    else:
        (xp_ref, xc_ref, xn_ref, dt_ref, cw_ref, cbias_ref, dtb_ref, alog_ref, e_ref,
         out_ref, h_scr) = refs
    q = SSM_CHUNK
    rows = SSD_STEP * q
    s = pl.program_id(1)
    n_ctx_s = n_ctx // SSD_STEP
    n_lat_s = n_lat // SSD_STEP
    is_ctx = s < n_ctx_s
    if backward:
        cc = jnp.where(is_ctx, n_ctx_s - 1 - s, n_lat_s - 1 - (s - n_ctx_s))
    else:
        cc = jnp.where(is_ctx, s, s - n_ctx_s)
    ncc = jnp.where(is_ctx, n_ctx_s, n_lat_s)
    at_start = cc == 0
    at_end = cc == ncc - 1

    @pl.when(s == 0)
    def _():
        h_scr[...] = jnp.zeros_like(h_scr)

    prev = jnp.where(at_start, 0.0, xp_ref[...])
    nxt = jnp.where(at_end, 0.0, xn_ref[...])
    ext = jnp.concatenate([prev, xc_ref[...], nxt], axis=0)
    acc = jnp.broadcast_to(cbias_ref[...], (rows, XBC_WIDTH))
    base = 8 - SSM_CONV // 2
    for j in range(SSM_CONV):
        acc = acc + cw_ref[j:j + 1, :] * ext[base + j:base + j + rows, :]
    u_all = _silu(acc)

    lane = lax.broadcasted_iota(jnp.int32, (rows, LANE), 1)
    dtr = dt_ref[...] + dtb_ref[...]
    dtv_all = jnp.maximum(dtr, 0.0) + jnp.log1p(jnp.exp(-jnp.abs(dtr)))
    dtv_all = jnp.where(lane < SSM_HEADS, dtv_all, 0.0)
    da_all = dtv_all * (-jnp.exp(alog_ref[...]))

    ri = lax.broadcasted_iota(jnp.int32, (q, q), 0)
    ci = lax.broadcasted_iota(jnp.int32, (q, q), 1)
    mask = (ci >= ri) if backward else (ci <= ri)
    tri = jnp.where(mask, 1.0, 0.0).astype(BF16)
    lane_w = lax.broadcasted_iota(jnp.int32, (q, SSM_INNER), 1) // SSM_HEAD_DIM
    last = 0 if backward else q - 1
    per_g = SSM_HEADS // SSM_GROUPS
    gw = per_g * SSM_HEAD_DIM

    h_cur = h_scr[...]
    for sub in (range(SSD_STEP - 1, -1, -1) if backward else range(SSD_STEP)):
        r0 = sub * q
        u = u_all[r0:r0 + q, :]
        xc = u[:, :SSM_INNER]
        dtv = dtv_all[r0:r0 + q, :]
        acs = _dot_01(tri, da_all[r0:r0 + q, :])
        acs_t = acs.T
        acs_last = acs[last:last + 1, :]
        small = jnp.concatenate([dtv, jnp.exp(acs_last - acs), jnp.exp(acs)], axis=0)
        wide = _dot_01(small, e_ref[...], ones_on_right=True)
        dt_w = wide[:q]
        decay_w = wide[q:2 * q]
        eacs_w = wide[2 * q:]
        elast_w = eacs_w[last:last + 1, :]

        xdt = xc * dt_w
        xw_b = (xdt * decay_w).astype(BF16)
        h_b = h_cur.astype(BF16)

        w_parts = []
        x_parts = []
        inter = []
        upd = []
        for g in range(SSM_GROUPS):
            b_g = u[:, SSM_INNER + g * SSM_STATE:SSM_INNER + (g + 1) * SSM_STATE].astype(BF16)
            o_c = SSM_INNER + SSM_GROUPS * SSM_STATE
            c_g = u[:, o_c + g * SSM_STATE:o_c + (g + 1) * SSM_STATE].astype(BF16)
            cbm = lax.dot_general(c_g, b_g, (((1,), (1,)), ((), ())), preferred_element_type=F32)
            for hh in range(per_g):
                hd = g * per_g + hh
                seg = acs[:, hd:hd + 1] - acs_t[hd:hd + 1, :]
                lmat = jnp.exp(jnp.where(mask, seg, -jnp.inf))
                w_parts.append((cbm * lmat).astype(BF16))
                x_parts.append(jnp.where(lane_w == hd, xdt, 0.0).astype(BF16))
            inter.append(jnp.dot(c_g, h_b[:, g * gw:(g + 1) * gw], preferred_element_type=F32))
            upd.append(lax.dot_general(b_g, xw_b[:, g * gw:(g + 1) * gw], (((0,), (0,)), ((), ())),
                                       preferred_element_type=F32))
        y = jnp.dot(jnp.concatenate(w_parts, axis=1), jnp.concatenate(x_parts, axis=0),
                    preferred_element_type=F32)
        y = y + jnp.concatenate(inter, axis=1) * eacs_w
        h_cur = h_cur * elast_w + jnp.concatenate(upd, axis=1)

        if backward:
            tot = yf_ref[r0:r0 + q, :] + y + skip_ref[...] * xc
            yz = tot * _silu(z_ref[r0:r0 + q, :])
            half = SSM_INNER // SSM_GROUPS
            outs = []
            for g in range(SSM_GROUPS):
                grp = yz[:, g * half:(g + 1) * half]
                outs.append(grp * lax.rsqrt(jnp.mean(grp * grp, axis=-1, keepdims=True) + RMS_EPS))
            out_ref[r0:r0 + q, :] = (jnp.concatenate(outs, axis=1) * gssm_ref[...]).astype(BF16)
        else:
            out_ref[r0:r0 + q, :] = y
    h_scr[...] = h_cur


def _ssd_sweep(backward, xbc, dtc, wp, dims, yf=None, z=None):
    nb = dims["batch"]
    n_lat = dims["seq"] // SSM_CHUNK
    n_ctx = dims["ctx"] // SSM_CHUNK
    assert n_lat % SSD_STEP == 0 and n_ctx % SSD_STEP == 0
    n_lat_s = n_lat // SSD_STEP
    n_ctx_s = n_ctx // SSD_STEP
    ctx_base = nb * n_lat_s
    rows = xbc.shape[0]
    step_rows = SSD_STEP * SSM_CHUNK
    sub = step_rows // 8

    def blk(b, s):
        is_ctx = s < n_ctx_s
        if backward:
            cc = jnp.where(is_ctx, n_ctx_s - 1 - s, n_lat_s - 1 - (s - n_ctx_s))
        else:
            cc = jnp.where(is_ctx, s, s - n_ctx_s)
        return jnp.where(is_ctx, ctx_base + b * n_ctx_s + cc, b * n_lat_s + cc)

    cur = lambda w: pl.BlockSpec((step_rows, w), lambda b, s: (blk(b, s), 0))
    d = 1 if backward else 0
    in_specs = [
        pl.BlockSpec((8, XBC_WIDTH), lambda b, s: (jnp.maximum(blk(b, s) * sub - 1, 0), 0)),
        cur(XBC_WIDTH),
        pl.BlockSpec((8, XBC_WIDTH), lambda b, s: (jnp.minimum((blk(b, s) + 1) * sub, rows // 8 - 1), 0)),
        pl.BlockSpec((step_rows, LANE), lambda b, s: (blk(b, s), d)),
        _resident((8, XBC_WIDTH)), _resident((1, XBC_WIDTH)),
        _resident((1, LANE)), _resident((1, LANE)), _resident((LANE, SSM_INNER)),
    ]
    args = [xbc, xbc, xbc, dtc, wp["conv_w"], wp["conv_b"], wp["dt_bias"][d], wp["a_log"][d], wp["expand"]]
    if backward:
        in_specs += [cur(SSM_INNER), cur(SSM_INNER), _resident((1, SSM_INNER)), _resident((1, SSM_INNER))]
        args += [yf, z, wp["d_skip"], wp["g_ssm"]]
    return pl.pallas_call(
        functools.partial(_ssd_kernel, backward, n_ctx, n_lat),
        grid=(nb, n_ctx_s + n_lat_s),
        in_specs=in_specs,
        out_specs=cur(SSM_INNER),
        out_shape=jax.ShapeDtypeStruct((rows, SSM_INNER), BF16 if backward else F32),
        scratch_shapes=[pltpu.VMEM((SSM_STATE, SSM_INNER), F32)],
        compiler_params=_cparams(("arbitrary", "arbitrary")),
        name="ssd_bwd" if backward else "ssd_fwd",
    )(*args)


def _attn_kernel(nk, tk, q_ref, k_ref, vt_ref, o_ref):
    q = q_ref[...]
    tq = q.shape[0]
    m = jnp.full((1, tq), -jnp.inf, F32)
    l = jnp.zeros((1, tq), F32)
    acc = jnp.zeros((V_DIM, tq), F32)
    for i in range(nk):
        k = k_ref[i * tk:(i + 1) * tk, :]
        vt = vt_ref[:, i * tk:(i + 1) * tk]
        st = lax.dot_general(k, q, (((1,), (1,)), ((), ())), preferred_element_type=F32)
        m_new = jnp.maximum(m, jnp.max(st, axis=0, keepdims=True))
        alpha = jnp.exp2(m - m_new)
        p = jnp.exp2(st - m_new)
        l = alpha * l + jnp.sum(p, axis=0, keepdims=True)
        acc = alpha * acc + jnp.dot(vt, p.astype(BF16), preferred_element_type=F32)
        m = m_new
    o_ref[...] = (acc / l).T.astype(BF16)


def _attention(q, k, vt, nb, lq, lk, q_row0, kv_stride, tq, tk):
    nq = lq // tq
    q0 = q_row0 // tq
    assert kv_stride % lk == 0 and lk % tk == 0 and q_row0 % tq == 0
    kvb = kv_stride // lk
    return pl.pallas_call(
        functools.partial(_attn_kernel, lk // tk, tk),
        grid=(nb, MLA_HEADS, nq),
        in_specs=[
            pl.BlockSpec((tq, HEAD_PAD), lambda b, h, i: (q0 + b * nq + i, h)),
            pl.BlockSpec((lk, HEAD_PAD), lambda b, h, i: (b * kvb, h)),
            pl.BlockSpec((V_DIM, lk), lambda b, h, i: (h, b * kvb)),
        ],
        out_specs=pl.BlockSpec((tq, V_DIM), lambda b, h, i: (b * nq + i, h)),
        out_shape=jax.ShapeDtypeStruct((nb * lq, MLA_HEADS * V_DIM), BF16),
        compiler_params=_cparams(("arbitrary", "arbitrary", "arbitrary")),
        name="mla_attention",
    )(q, k, vt)


def _out_proj_kernel(n_lat_tiles, gm_ref, ssm_ref, att_ref, attc_ref, x_ref, mod_ref, w_ref, g_ref, b_ref, wr_ref,
                     x1_ref, tok_ref, lg_ref):
    mod = mod_ref[0]
    for r0 in range(0, x_ref.shape[0], ROW_TILE):
        rows = slice(r0, r0 + ROW_TILE)
        att = att_ref[rows, :]
        if attc_ref is not None:
            att = jnp.where(pl.program_id(0) >= n_lat_tiles, attc_ref[rows, :], att)
        mixed = jnp.concatenate([gm_ref[rows, :], ssm_ref[rows, :], att], axis=1)
        mix = jnp.dot(mixed, w_ref[...], preferred_element_type=F32)
        x1 = _layer_norm(DN_ALPHA * x_ref[rows, :] + mod[2:3, :] * mix, g_ref[...], b_ref[...])
        x1_ref[rows, :] = x1
        tok = x1 * (1.0 + mod[4:5, :]) + mod[3:4, :]
        tok_ref[rows, :] = _pack_rows(tok)
        lg_ref[:, rows] = lax.dot_general(wr_ref[...], tok, (((1,), (1,)), ((), ())),
                                          precision=HIGHEST, preferred_element_type=F32)


def _out_proj_lat_kernel(n_lat_tiles, gm_ref, ssm_ref, att_ref, *rest):
    _out_proj_kernel(n_lat_tiles, gm_ref, ssm_ref, att_ref, None, *rest)


def _out_proj(gm, ssm, att, att_ctx, x_all, modv, wp, dims, rows):
    tm = OUT_TILE
    assert dims["seq"] % tm == 0 and (dims["batch"] * dims["ctx"]) % tm == 0
    lat_tiles = dims["seq"] // tm
    nb = dims["batch"]
    n_lat = nb * lat_tiles
    row = lambda w: pl.BlockSpec((tm, w), lambda i: (i, 0))
    att_specs = [pl.BlockSpec((tm, MLA_HEADS * V_DIM), lambda i: (jnp.minimum(i, n_lat - 1), 0))]
    att_args = [att]
    body = _out_proj_lat_kernel
    if att_ctx is not None:
        att_specs.append(pl.BlockSpec((tm, MLA_HEADS * V_DIM), lambda i: (jnp.maximum(i - n_lat, 0), 0)))
        att_args.append(att_ctx)
        body = _out_proj_kernel
    return pl.pallas_call(
        functools.partial(body, n_lat),
        grid=(rows // tm,),
        in_specs=[
            row(GM_WIDTH), row(SSM_INNER), *att_specs, row(D_MODEL),
            pl.BlockSpec((1, 6, D_MODEL), lambda i: (jnp.minimum(i // lat_tiles, nb), 0, 0)),
            _resident((D_MODEL, D_MODEL)), _resident((1, D_MODEL)), _resident((1, D_MODEL)),
            _resident((N_EXPERTS, D_MODEL)),
        ],
        out_specs=[row(D_MODEL), row(D_MODEL // 2), pl.BlockSpec((N_EXPERTS, tm), lambda i: (0, i))],
        out_shape=[
            jax.ShapeDtypeStruct((rows, D_MODEL), F32),
            jax.ShapeDtypeStruct((rows, D_MODEL // 2), jnp.uint32),
            jax.ShapeDtypeStruct((N_EXPERTS, rows), F32),
        ],
        compiler_params=_cparams(("arbitrary",)),
        name="out_proj",
    )(gm, ssm, *att_args, x_all, modv, wp["w_out"], wp["ln1_g"], wp["ln1_b"], wp["w_router_t"])


def _gate_kernel(lg_ref, b_ref, eidx_ref, w_ref, pos_ref, cnt_ref, carry_scr):
    per = N_EXPERTS // N_EXPERT_GROUPS
    tn = lg_ref.shape[1]
    sc = jax.nn.sigmoid(lg_ref[...])
    bi = sc + b_ref[...]
    neg = np.float32(-np.inf)
    xs = [bi[j * N_EXPERT_GROUPS:(j + 1) * N_EXPERT_GROUPS, :] for j in range(per)]
    ss = [sc[j * N_EXPERT_GROUPS:(j + 1) * N_EXPERT_GROUPS, :] for j in range(per)]
    m1 = functools.reduce(jnp.maximum, xs)
    first = functools.reduce(jnp.minimum, [jnp.where(xs[j] == m1, j, per) for j in range(per)])
    m2 = functools.reduce(jnp.maximum, [jnp.where(first == j, neg, xs[j]) for j in range(per)])
    gscore = m1 + m2
    gi = lax.broadcasted_iota(jnp.int32, (N_EXPERT_GROUPS, tn), 0)
    cur = gscore
    gsel = jnp.zeros((N_EXPERT_GROUPS, tn), jnp.int32)
    for _ in range(TOPK_GROUPS):
        mx = jnp.max(cur, axis=0, keepdims=True)
        fi = jnp.min(jnp.where(cur == mx, gi, N_EXPERT_GROUPS), axis=0, keepdims=True)
        pick = gi == fi
        gsel = jnp.where(pick, 1, gsel)
        cur = jnp.where(pick, neg, cur)
    cur = [jnp.where(gsel > 0, xs[j], neg) for j in range(per)]
    eid = [gi * per + j for j in range(per)]
    sel = [jnp.zeros((N_EXPERT_GROUPS, tn), F32) for _ in range(per)]
    picked = []
    for _ in range(TOP_K):
        mx = jnp.max(functools.reduce(jnp.maximum, cur), axis=0, keepdims=True)
        cand = functools.reduce(jnp.minimum, [jnp.where(cur[j] == mx, eid[j], N_EXPERTS) for j in range(per)])
        fi = jnp.min(cand, axis=0, keepdims=True)
        wk = jnp.zeros((N_EXPERT_GROUPS, tn), F32)
        for j in range(per):
            pick = eid[j] == fi
            sel[j] = jnp.where(pick, 1.0, sel[j])
            cur[j] = jnp.where(pick, neg, cur[j])
            wk = wk + jnp.where(pick, ss[j], 0.0)
        picked.append((fi, jnp.sum(wk, axis=0, keepdims=True)))
    tot = functools.reduce(jnp.add, [wk for _, wk in picked])

    @pl.when(pl.program_id(0) == 0)
    def _():
        carry_scr[...] = jnp.zeros_like(carry_scr)

    sel_all = jnp.concatenate(sel, axis=0)
    ri = lax.broadcasted_iota(jnp.int32, (tn, tn), 0)
    ci = lax.broadcasted_iota(jnp.int32, (tn, tn), 1)
    before = jnp.where(ri < ci, 1.0, 0.0).astype(BF16)
    prefix = jnp.dot(sel_all.astype(BF16), before, preferred_element_type=F32)
    carry = carry_scr[...]
    rank = prefix + carry
    carry_new = carry + prefix[:, tn - 1:tn] + sel_all[:, tn - 1:tn]
    carry_scr[...] = carry_new
    cnt_ref[...] = jnp.broadcast_to(carry_new, cnt_ref.shape).astype(jnp.int32)
    for k, (fi, wk) in enumerate(picked):
        pk = jnp.zeros((N_EXPERT_GROUPS, tn), F32)
        for j in range(per):
            pk = pk + jnp.where(eid[j] == fi, rank[j * N_EXPERT_GROUPS:(j + 1) * N_EXPERT_GROUPS, :], 0.0)
        eidx_ref[k:k + 1, :] = fi
        w_ref[k:k + 1, :] = wk / tot * ROUTED_SCALE
        pos_ref[k:k + 1, :] = jnp.sum(pk, axis=0, keepdims=True).astype(jnp.int32)


def _gate(logits_t, b_router_col):
    t = logits_t.shape[1]
    tn = 512
    tile = pl.BlockSpec((TOP_K, tn), lambda i: (0, i))
    return pl.pallas_call(
        _gate_kernel,
        grid=(t // tn,),
        in_specs=[pl.BlockSpec((N_EXPERTS, tn), lambda i: (0, i)), _resident((N_EXPERTS, 1))],
        out_specs=[tile, tile, tile, pl.BlockSpec((N_EXPERTS, LANE), lambda i: (0, 0))],
        out_shape=[
            jax.ShapeDtypeStruct((TOP_K, t), jnp.int32),
            jax.ShapeDtypeStruct((TOP_K, t), F32),
            jax.ShapeDtypeStruct((TOP_K, t), jnp.int32),
            jax.ShapeDtypeStruct((N_EXPERTS, LANE), jnp.int32),
        ],
        scratch_shapes=[pltpu.VMEM((N_EXPERTS, 1), F32)],
        compiler_params=_cparams(("arbitrary",)),
        name="gate_topk",
    )(logits_t, b_router_col)


DISPATCH_TILE = 512


def _dispatch_kernel(padlo_ref, padn_ref, nu_ref, slot_ref, tok_ref, xs_hbm, zero_buf, sem, zsem):
    i = pl.program_id(0)
    td = slot_ref.shape[1]

    def row_copy(r, k):
        return pltpu.make_async_copy(tok_ref.at[pl.ds(r, 1)], xs_hbm.at[pl.ds(slot_ref[k, r], 1)], sem)

    def issue(r, carry):
        for k in range(TOP_K):
            row_copy(r, k).start(priority=k % 2)
        return carry

    lax.fori_loop(0, td, issue, 0)

    @pl.when(i == 0)
    def _():
        zero_buf[...] = jnp.zeros_like(zero_buf)

        def fill(e, carry):
            lo = padlo_ref[e]
            n = padn_ref[e]
            head = jnp.minimum((-lo) & 7, n)

            for s in range(7):
                @pl.when(s < head)
                def _(s=s):
                    cp = pltpu.make_async_copy(zero_buf.at[pl.ds(0, 1)], xs_hbm.at[pl.ds(lo + s, 1)], zsem)
                    cp.start()
                    cp.wait()

            rest = n - head
            off = lo + head
            size = EXPERT_BLOCK // 2
            while size >= 8:
                hit = (rest & size) != 0

                @pl.when(hit)
                def _(off=off, size=size):
                    dst = xs_hbm.at[pl.ds(pl.multiple_of(off, 8), size)]
                    cp = pltpu.make_async_copy(zero_buf.at[pl.ds(0, size)], dst, zsem)
                    cp.start()
                    cp.wait()

                off = off + jnp.where(hit, size, 0)
                size //= 2
            return carry

        lax.fori_loop(0, N_EXPERTS, fill, 0)

        zrows = zero_buf.shape[0]

        def fill_tail(b, carry):
            for h in range(EXPERT_BLOCK // zrows):
                row0 = pl.multiple_of(b * EXPERT_BLOCK + h * zrows, 8)
                cp = pltpu.make_async_copy(zero_buf, xs_hbm.at[pl.ds(row0, zrows)], zsem)
                cp.start()
                cp.wait()
            return carry

        lax.fori_loop(nu_ref[0], xs_hbm.shape[0] // EXPERT_BLOCK, fill_tail, 0)

    for _ in range(TOP_K):
        pltpu.make_async_copy(tok_ref, xs_hbm.at[pl.ds(0, td)], sem).wait()


def _dispatch_rows(slot, pad_lo, pad_n, n_used, tok, cap):
    t = tok.shape[0]
    td = DISPATCH_TILE
    return pl.pallas_call(
        _dispatch_kernel,
        grid_spec=pltpu.PrefetchScalarGridSpec(
            num_scalar_prefetch=3,
            grid=(t // td,),
            in_specs=[
                pl.BlockSpec((TOP_K, td), lambda i, lo, n, nu: (0, i), memory_space=pltpu.SMEM),
                pl.BlockSpec((td, D_MODEL // 2), lambda i, lo, n, nu: (i, 0)),
            ],
            out_specs=pl.BlockSpec(memory_space=pl.ANY),
            scratch_shapes=[pltpu.VMEM((EXPERT_BLOCK // 2, D_MODEL // 2), jnp.uint32),
                            pltpu.SemaphoreType.DMA, pltpu.SemaphoreType.DMA],
        ),
        out_shape=jax.ShapeDtypeStruct((cap, D_MODEL // 2), jnp.uint32),
        compiler_params=_cparams(("arbitrary",)),
        name="dispatch_rows",
    )(pad_lo, pad_n, n_used, slot, tok)


def _experts_kernel(be_ref, nu_ref, x_ref, wg_ref, wu_ref, wd_ref, y_ref, wg_s, wu_s, wd_s):
    j = pl.program_id(0)
    prev = be_ref[jnp.maximum(j - 1, 0)]
    fresh = jnp.logical_or(j == 0, be_ref[j] != prev)

    @pl.when(fresh)
    def _():
        wg_s[...] = wg_ref[0].astype(BF16)
        wu_s[...] = wu_ref[0].astype(BF16)
        wd_s[...] = wd_ref[0].astype(BF16)

    @pl.when(j < nu_ref[0])
    def _():
        x = _unpack_rows(x_ref[...])
        g = jnp.dot(x, wg_s[...], preferred_element_type=F32)
        u = jnp.dot(x, wu_s[...], preferred_element_type=F32)
        hb = (_silu(g) * u).astype(BF16)
        y_ref[...] = _pack_rows(jnp.dot(hb, wd_s[...], preferred_element_type=F32))

    @pl.when(j >= nu_ref[0])
    def _():
        y_ref[...] = jnp.zeros_like(y_ref)


def _experts(layer, block_e, n_used, x_sorted, w_gate, w_up, w_down):
    cap = x_sorted.shape[0]
    blk = EXPERT_BLOCK
    return pl.pallas_call(
        _experts_kernel,
        grid_spec=pltpu.PrefetchScalarGridSpec(
            num_scalar_prefetch=2,
            grid=(cap // blk,),
            in_specs=[
                pl.BlockSpec((blk, D_MODEL // 2), lambda j, be, nu: (jnp.minimum(j, nu[0] - 1), 0)),
                pl.BlockSpec((None, 1, D_MODEL, EXPERT_FF), lambda j, be, nu: (layer, be[j], 0, 0)),
                pl.BlockSpec((None, 1, D_MODEL, EXPERT_FF), lambda j, be, nu: (layer, be[j], 0, 0)),
                pl.BlockSpec((None, 1, EXPERT_FF, D_MODEL), lambda j, be, nu: (layer, be[j], 0, 0)),
            ],
            out_specs=pl.BlockSpec((blk, D_MODEL // 2), lambda j, be, nu: (j, 0)),
            scratch_shapes=[pltpu.VMEM((D_MODEL, EXPERT_FF), BF16), pltpu.VMEM((D_MODEL, EXPERT_FF), BF16),
                            pltpu.VMEM((EXPERT_FF, D_MODEL), BF16)],
        ),
        out_shape=jax.ShapeDtypeStruct((cap, D_MODEL // 2), jnp.uint32),
        compiler_params=_cparams(("arbitrary",)),
        name="routed_experts",
    )(block_e, n_used, x_sorted, w_gate, w_up, w_down)


COMBINE_TILE = 256


def _final_kernel(slot_ref, slot_next_ref, w8_ref, tok_ref, x1_ref, mod_ref, wgu_ref, wd_ref, g_ref, b_ref,
                  y_hbm, o_ref, ybuf, sems):
    i = pl.program_id(0)
    n = pl.num_programs(0)
    tf = tok_ref.shape[0]
    cur = i % 2

    def gather(slots, buf):
        def issue(r, carry):
            for k in range(TOP_K):
                pltpu.make_async_copy(y_hbm.at[pl.ds(slots[k, r], 1)], ybuf.at[buf, k, pl.ds(r, 1)],
                                      sems.at[buf]).start(priority=k % 2)
            return carry

        lax.fori_loop(0, tf, issue, 0)

    @pl.when(i == 0)
    def _():
        gather(slot_ref, 0)

    @pl.when(i + 1 < n)
    def _():
        gather(slot_next_ref, 1 - cur)

    mod = mod_ref[0]
    gu = jnp.dot(_unpack_rows(tok_ref[...]), wgu_ref[...], preferred_element_type=F32)
    hb = (_silu(gu[:, :SHARED_FF]) * gu[:, SHARED_FF:]).astype(BF16)
    f = jnp.dot(hb, wd_ref[...], preferred_element_type=F32)

    for k in range(TOP_K):
        pltpu.make_async_copy(y_hbm.at[pl.ds(0, tf)], ybuf.at[cur, k], sems.at[cur]).wait()
    w8 = w8_ref[...]
    half = D_MODEL // 2
    f_lo = f[:, :half]
    f_hi = f[:, half:]
    for k in range(TOP_K):
        yk = ybuf[cur, k]
        wk = w8[:, k:k + 1]
        f_lo = f_lo + wk * pltpu.bitcast(yk << 16, F32)
        f_hi = f_hi + wk * pltpu.bitcast(yk & jnp.uint32(0xFFFF0000), F32)
    f = jnp.concatenate([f_lo, f_hi], axis=1)
    o_ref[...] = _layer_norm(DN_ALPHA * x1_ref[...] + mod[5:6, :] * f, g_ref[...], b_ref[...])


def _final(slot, w8_t, tok, y_sorted, x1, modv, wp, dims, rows):
    tf = COMBINE_TILE
    nt = rows // tf
    lat_tiles = dims["seq"] // tf
    nb = dims["batch"]
    row = lambda w: pl.BlockSpec((tf, w), lambda i: (i, 0))
    return pl.pallas_call(
        _final_kernel,
        grid=(nt,),
        in_specs=[
            pl.BlockSpec((TOP_K, tf), lambda i: (0, i), memory_space=pltpu.SMEM),
            pl.BlockSpec((TOP_K, tf), lambda i: (0, jnp.minimum(i + 1, nt - 1)), memory_space=pltpu.SMEM),
            row(TOP_K), row(D_MODEL // 2), row(D_MODEL),
            pl.BlockSpec((1, 6, D_MODEL), lambda i: (jnp.minimum(i // lat_tiles, nb), 0, 0)),
            _resident((D_MODEL, 2 * SHARED_FF)), _resident((SHARED_FF, D_MODEL)),
            _resident((1, D_MODEL)), _resident((1, D_MODEL)),
            pl.BlockSpec(memory_space=pl.ANY),
        ],
        out_specs=row(D_MODEL),
        out_shape=jax.ShapeDtypeStruct((rows, D_MODEL), F32),
        scratch_shapes=[pltpu.VMEM((2, TOP_K, tf, D_MODEL // 2), jnp.uint32), pltpu.SemaphoreType.DMA((2,))],
        compiler_params=_cparams(("arbitrary",)),
        name="combine_shared_ln2",
    )(slot, slot, w8_t, tok, x1, modv, wp["w_sh_gu"], wp["w_sh_down"], wp["ln2_g"], wp["ln2_b"], y_sorted)


def _rot_half_cols():
    src = np.zeros((QK_ROPE,), np.int32)
    sign = np.zeros((QK_ROPE,), np.float32)
    quarter = QK_ROPE // 4
    for dcol in range(QK_ROPE):
        part, i = divmod(dcol, 2 * quarter)
        half, kk = divmod(i, quarter)
        src[dcol] = part * 2 * quarter + (quarter + kk if half == 0 else kk)
        sign[dcol] = -1.0 if half == 0 else 1.0
    return src, sign


def _expert_major(a, axis):
    per = N_EXPERTS // N_EXPERT_GROUPS
    shp = a.shape
    a = a.reshape(shp[:axis] + (N_EXPERT_GROUPS, per) + shp[axis + 1:])
    a = jnp.swapaxes(a, axis, axis + 1)
    return a.reshape(shp)


def _prep_layer(l, p):
    src, sign = _rot_half_cols()
    w_in = p["w_in"][l]
    col = lambda lo, n: w_in[:, lo:lo + n]
    zpad = jnp.zeros((D_MODEL, LANE - SSM_HEADS), F32)
    w_kr = col(R_KR, QK_ROPE)
    w_in_p = jnp.concatenate([
        col(R_GM, 2 * GM_WIDTH), col(R_Q, Q_LORA), col(R_Z, SSM_INNER), col(R_XBC, XBC_WIDTH),
        col(R_DT, SSM_HEADS), zpad, col(R_DT + SSM_HEADS, SSM_HEADS), zpad,
        col(R_KV, KV_LORA), w_kr, w_kr[:, src] * sign,
    ], axis=1).astype(BF16)
    wq = p["w_q_b"][l].reshape(Q_LORA, MLA_HEADS, QK_NOPE + QK_ROPE)
    wq_rope = wq[:, :, QK_NOPE:]
    w_q = jnp.concatenate([wq[:, :, :QK_NOPE], wq_rope, wq_rope[:, :, src] * sign], axis=2)
    w_q = w_q.reshape(Q_LORA, MLA_HEADS * HEAD_PAD).astype(BF16)
    wkv = p["w_kv_b"][l].reshape(KV_LORA, MLA_HEADS, QK_NOPE + V_DIM)
    w_k = wkv[:, :, :QK_NOPE].reshape(KV_LORA, -1).astype(BF16)
    w_vt = wkv[:, :, QK_NOPE:].reshape(KV_LORA, -1).T.astype(BF16)
    lane_pad = lambda a: jnp.pad(a, ((0, 0), (0, LANE - a.shape[1])))
    expand = jnp.repeat(jnp.eye(LANE, SSM_HEADS, dtype=BF16), SSM_HEAD_DIM, axis=1)
    r2 = lambda a: a.reshape(1, -1)
    return dict(
        w_in=w_in_p, g_v=r2(p["g_v"][l]), b_v=r2(p["b_v"][l]), w_sp=p["w_sp"][l].astype(BF16),
        b_sp=jnp.repeat(p["b_sp"][l].T, GM_CHUNK, axis=1),
        g_q=r2(p["g_q"][l]), w_q=w_q, g_kv=r2(p["g_kv"][l]), w_k=w_k, w_vt=w_vt,
        conv_w=jnp.pad(p["conv_w"][l], ((0, 8 - SSM_CONV), (0, 0))), conv_b=r2(p["conv_b"][l]),
        dt_bias=lane_pad(p["dt_bias"][l])[:, None, :], a_log=lane_pad(p["a_log"][l])[:, None, :],
        expand=expand, d_skip=r2(jnp.repeat(p["d_skip"][l], SSM_HEAD_DIM)), g_ssm=r2(p["g_ssm"][l]),
        w_out=p["w_out"][l].astype(BF16), ln1_g=r2(p["ln1_g"][l]), ln1_b=r2(p["ln1_b"][l]),
        ln2_g=r2(p["ln2_g"][l]), ln2_b=r2(p["ln2_b"][l]),
        w_router_t=_expert_major(p["w_router"][l].T, 0),
        b_router=_expert_major(p["b_router"][l], 0).reshape(N_EXPERTS, 1),
        w_sh_gu=jnp.concatenate([p["w_sh_gate"][l], p["w_sh_up"][l]], axis=1).astype(BF16),
        w_sh_down=p["w_sh_down"][l].astype(BF16),
    )


def _rope_tables(nb, seq, nctx):
    t = jnp.arange(seq)
    half = QK_ROPE // 2
    inv = 1.0 / (ROPE_BASE ** (jnp.arange(0, half, 2, dtype=F32) / half))
    ang_r = (t // GRID_W)[:, None] * inv
    ang_c = (t % GRID_W)[:, None] * inv
    ang = jnp.concatenate([ang_r, ang_r, ang_c, ang_c], axis=1)
    zero = jnp.zeros((seq, LANE - QK_ROPE), F32)
    ca_lat = jnp.concatenate([jnp.cos(ang), zero], axis=1)
    cb_lat = jnp.concatenate([jnp.sin(ang), zero], axis=1)
    ca_ctx = jnp.concatenate([jnp.ones((nb * nctx, QK_ROPE), F32), jnp.zeros((nb * nctx, LANE - QK_ROPE), F32)], axis=1)
    ca = jnp.concatenate([jnp.tile(ca_lat, (nb, 1)), ca_ctx], axis=0)
    cb = jnp.concatenate([jnp.tile(cb_lat, (nb, 1)), jnp.zeros((nb * nctx, LANE), F32)], axis=0)
    return ca, cb


def _slot_layout(eidx, pos, cnt, t):
    per = N_EXPERTS // N_EXPERT_GROUPS
    blk = EXPERT_BLOCK
    counts = cnt[:, 0].reshape(per, N_EXPERT_GROUPS).T.reshape(N_EXPERTS)
    padded = (counts + blk - 1) // blk * blk
    pend = jnp.cumsum(padded)
    pstart = pend - padded
    n_blocks = (t * TOP_K + N_EXPERTS * (blk - 1) + blk - 1) // blk
    block_e = jnp.sum((pend[None, :] <= (jnp.arange(n_blocks) * blk)[:, None]).astype(jnp.int32), axis=1)
    block_e = jnp.minimum(block_e, N_EXPERTS - 1)
    n_used = (pend[-1] // blk).reshape(1)
    hit = eidx[:, :, None] == jnp.arange(N_EXPERTS, dtype=jnp.int32)
    slot = (jnp.sum(jnp.where(hit, pstart, 0), axis=2) + pos).astype(jnp.int32)
    return (slot, block_e.astype(jnp.int32), n_used.astype(jnp.int32), (pstart + counts).astype(jnp.int32),
            (padded - counts).astype(jnp.int32), n_blocks * blk)


def kernel(x, c, ctx, c_ctx, w_mod, b_mod, w_in, g_q, w_q_b, g_kv, w_kv_b, conv_w, conv_b, a_log, dt_bias, d_skip, g_ssm, g_v, b_v, w_sp, b_sp, w_out, ln1_g, ln1_b, ln2_g, ln2_b, w_router, b_router, w_e_gate, w_e_up, w_e_down, w_sh_gate, w_sh_up, w_sh_down):
    p = dict(w_in=w_in, g_q=g_q, w_q_b=w_q_b, g_kv=g_kv, w_kv_b=w_kv_b, conv_w=conv_w, conv_b=conv_b,
             a_log=a_log, dt_bias=dt_bias, d_skip=d_skip, g_ssm=g_ssm, g_v=g_v, b_v=b_v, w_sp=w_sp, b_sp=b_sp,
             w_out=w_out, ln1_g=ln1_g, ln1_b=ln1_b, ln2_g=ln2_g, ln2_b=ln2_b, w_router=w_router,
             b_router=b_router, w_sh_gate=w_sh_gate, w_sh_up=w_sh_up, w_sh_down=w_sh_down)
    nb, seq, d = x.shape
    nctx = ctx.shape[1]
    depth = w_mod.shape[0]
    assert d == D_MODEL and depth == DEPTH and nb + 1 <= 8
    assert seq % ROW_TILE == 0 and nctx % ROW_TILE == 0 and seq % GRID_W == 0
    dims = dict(batch=nb, seq=seq, ctx=nctx)
    lat_rows = nb * seq
    rows_all = lat_rows + nb * nctx
    lk = seq + nctx

    x_all = jnp.concatenate([x.reshape(lat_rows, d), ctx.reshape(nb * nctx, d)], axis=0)
    cvec = jnp.zeros((8, d), F32).at[:nb].set(c).at[nb].set(c_ctx)
    mods = _modulation(cvec, w_mod, b_mod).reshape(depth, 8, 6, d)[:, :nb + 1]
    ca, cb = _rope_tables(nb, seq, nctx)
    tq = min(1024, seq)
    tk = 768 if lk % 768 == 0 else ROW_TILE

    for l in range(depth):
        last = l == depth - 1
        wp = _prep_layer(l, p)
        modv = mods[l]
        gm, z, xbc, dtc, q, k, vt = _proj_in(x_all, modv, ca, cb, wp, dims)
        yf = _ssd_sweep(False, xbc, dtc, wp, dims)
        ssm = _ssd_sweep(True, xbc, dtc, wp, dims, yf=yf, z=z)
        att = _attention(q, k, vt, nb, seq, lk, 0, lk, tq, tk)
        rows = lat_rows if last else rows_all
        att_ctx = None if last else _attention(q, k, vt, nb, nctx, nctx, lat_rows, lk, ROW_TILE, ROW_TILE)
        x1, tok, logits_t = _out_proj(gm, ssm, att, att_ctx, x_all, modv, wp, dims, rows)
        eidx, w8, pos, cnt = _gate(logits_t, wp["b_router"])
        slot, block_e, n_used, pad_lo, pad_n, cap = _slot_layout(eidx, pos, cnt, rows)
        x_sorted = _dispatch_rows(slot, pad_lo, pad_n, n_used, tok, cap)
        y_sorted = _experts(l, block_e, n_used, x_sorted, w_e_gate, w_e_up, w_e_down)
        x_all = _final(slot, w8.T, tok, y_sorted, x1, modv, wp, dims, rows)
    return x_all.reshape(nb, seq, d)
```

```python
import functools
import math

import jax
import jax.numpy as jnp
import numpy as np
from jax import lax
from jax.experimental import pallas as pl
from jax.experimental.pallas import tpu as pltpu

F32 = jnp.float32
BF16 = jnp.bfloat16
HIGHEST = lax.Precision.HIGHEST

D_MODEL = 2048
GRID_W = 64
GM_HEADS = 4
GM_WIDTH = 512
GM_CHUNK = 128
SSM_HEADS = 8
SSM_HEAD_DIM = 64
SSM_INNER = 512
SSM_GROUPS = 2
SSM_STATE = 128
SSM_CONV = 5
SSM_CHUNK = 128
XBC_WIDTH = 1024
MLA_HEADS = 8
QK_NOPE = 128
QK_ROPE = 64
V_DIM = 128
Q_LORA = 768
KV_LORA = 256
ROPE_BASE = 10000.0
N_EXPERTS = 64
TOP_K = 8
N_EXPERT_GROUPS = 8
TOPK_GROUPS = 4
EXPERT_FF = 512
SHARED_FF = 512
ROUTED_SCALE = 2.5
LN_EPS = 1e-5
RMS_EPS = 1e-6
DEPTH = 2
DN_ALPHA = (2 * DEPTH) ** 0.25

R_GM = 0
R_Q = R_GM + 2 * GM_WIDTH
R_Z = R_Q + Q_LORA
R_XBC = R_Z + SSM_INNER
R_DT = R_XBC + XBC_WIDTH
R_KV = R_DT + 2 * SSM_HEADS
R_KR = R_KV + KV_LORA

LANE = 128
P_GM = 0
P_Q = P_GM + 2 * GM_WIDTH
P_Z = P_Q + Q_LORA
P_XBC = P_Z + SSM_INNER
P_DT = P_XBC + XBC_WIDTH
P_KV = P_DT + 2 * LANE
P_KR = P_KV + KV_LORA
P_WIDTH = P_KR + LANE

HEAD_PAD = 256
ROW_TILE = 256
OUT_TILE = 2 * ROW_TILE
SSD_STEP = 2
EXPERT_BLOCK = 512
VMEM_LIMIT = 56 * 1024 * 1024


def _cparams(sem):
    return pltpu.CompilerParams(dimension_semantics=sem, vmem_limit_bytes=VMEM_LIMIT)


def _resident(shape):
    n = len(shape)
    return pl.BlockSpec(shape, lambda *_: (0,) * n, pipeline_mode=pl.Buffered(1))


def _silu(x):
    return x * jax.nn.sigmoid(x)


def _layer_norm(y, g, b):
    mu = jnp.mean(y, axis=-1, keepdims=True)
    yc = y - mu
    var = jnp.mean(yc * yc, axis=-1, keepdims=True)
    return yc * lax.rsqrt(var + LN_EPS) * g + b


def _rms_norm(y, g):
    return y * lax.rsqrt(jnp.mean(y * y, axis=-1, keepdims=True) + RMS_EPS) * g


def _dot_01(a, b, ones_on_right=False):
    val, sel = (a, b) if ones_on_right else (b, a)
    out = None
    for _ in range(3):
        piece = val.astype(BF16)
        term = (jnp.dot(piece, sel, preferred_element_type=F32) if ones_on_right
                else jnp.dot(sel, piece, preferred_element_type=F32))
        out = term if out is None else out + term
        val = val - piece.astype(F32)
    return out


def _pack_rows(t):
    n = t.shape[1] // 2
    bits = pltpu.bitcast(t.astype(BF16).astype(F32), jnp.uint32)
    return (bits[:, :n] >> 16) | bits[:, n:]


def _unpack_rows(w):
    lo = pltpu.bitcast(w << 16, F32).astype(BF16)
    hi = pltpu.bitcast(w & jnp.uint32(0xFFFF0000), F32).astype(BF16)
    return jnp.concatenate([lo, hi], axis=1)


def _mod_kernel(c_ref, w_ref, b_ref, o_ref):
    c = c_ref[...]
    o_ref[0] = jnp.dot(_silu(c), w_ref[0], precision=HIGHEST, preferred_element_type=F32) + b_ref[0]


def _modulation(cvec, w_mod, b_mod):
    depth, d, n = w_mod.shape
    tn = 1536
    return pl.pallas_call(
        _mod_kernel,
        grid=(depth, n // tn),
        in_specs=[
            pl.BlockSpec((8, d), lambda l, j: (0, 0)),
            pl.BlockSpec((1, d, tn), lambda l, j: (l, 0, j)),
            pl.BlockSpec((1, 1, tn), lambda l, j: (l, 0, j)),
        ],
        out_specs=pl.BlockSpec((1, 8, tn), lambda l, j: (l, 0, j)),
        out_shape=jax.ShapeDtypeStruct((depth, 8, n), F32),
        compiler_params=_cparams(("arbitrary", "arbitrary")),
        name="modulation",
    )(cvec, w_mod, b_mod.reshape(depth, 1, n))


def _proj_in_kernel(x_ref, mod_ref, ca_ref, cb_ref, w_ref, gv_ref, bv_ref, wsp_ref, bsp_ref,
                    gq_ref, wq_ref, gkv_ref, wk_ref, wvt_ref,
                    gm_ref, z_ref, xbc_ref, dt_ref, q_ref, k_ref, vt_ref):
    tm = x_ref.shape[0]
    mod = mod_ref[0]
    h = (x_ref[...] * (1.0 + mod[1:2, :]) + mod[0:1, :]).astype(BF16)

    def proj(lo, n):
        return jnp.dot(h, w_ref[:, lo:lo + n], preferred_element_type=F32)

    uv = proj(P_GM, 2 * GM_WIDTH)
    gl = 0.5 * uv * (1.0 + lax.erf(uv * np.float32(math.sqrt(0.5))))
    u = gl[:, :GM_WIDTH]
    vn = _layer_norm(gl[:, GM_WIDTH:], gv_ref[...], bv_ref[...]).astype(BF16)
    for c in range(tm // GM_CHUNK):
        r0 = c * GM_CHUNK
        parts = [jnp.dot(wsp_ref[hh], vn[r0:r0 + GM_CHUNK, hh * LANE:(hh + 1) * LANE],
                         preferred_element_type=F32) for hh in range(GM_HEADS)]
        s = jnp.concatenate(parts, axis=1) + bsp_ref[...]
        gm_ref[r0:r0 + GM_CHUNK, :] = (u[r0:r0 + GM_CHUNK, :] * s).astype(BF16)

    z_ref[...] = proj(P_Z, SSM_INNER)
    xbc_ref[...] = proj(P_XBC, XBC_WIDTH)
    dt_ref[...] = proj(P_DT, 2 * LANE)

    ca = ca_ref[...]
    cb = cb_ref[...]
    scale = np.float32((QK_NOPE + QK_ROPE) ** -0.5 * math.log2(math.e))

    def rope(t):
        return t * ca + pltpu.roll(t, QK_ROPE, 1) * cb

    cqn = _rms_norm(proj(P_Q, Q_LORA), gq_ref[...]).astype(BF16)
    yq = jnp.dot(cqn, wq_ref[...], preferred_element_type=F32)
    for hh in range(MLA_HEADS):
        c0 = hh * HEAD_PAD
        q_ref[:, c0:c0 + LANE] = (yq[:, c0:c0 + LANE] * scale).astype(BF16)
        q_ref[:, c0 + LANE:c0 + HEAD_PAD] = (rope(yq[:, c0 + LANE:c0 + HEAD_PAD]) * scale).astype(BF16)

    ckvn = _rms_norm(proj(P_KV, KV_LORA), gkv_ref[...]).astype(BF16)
    krf = rope(proj(P_KR, LANE)).astype(BF16)
    yk = jnp.dot(ckvn, wk_ref[...], preferred_element_type=F32)
    for hh in range(MLA_HEADS):
        c0 = hh * HEAD_PAD
        k_ref[:, c0:c0 + LANE] = yk[:, hh * LANE:(hh + 1) * LANE].astype(BF16)
        k_ref[:, c0 + LANE:c0 + HEAD_PAD] = krf
    vt_ref[...] = lax.dot_general(wvt_ref[...], ckvn, (((1,), (1,)), ((), ())),
                                  preferred_element_type=F32).astype(BF16)


def _proj_in(x_all, modv, ca, cb, wp, dims):
    rows = x_all.shape[0]
    tm = ROW_TILE
    nt = rows // tm
    lat_tiles = dims["seq"] // tm
    ctx_tiles = dims["ctx"] // tm
    kv_tiles = lat_tiles + ctx_tiles
    nb = dims["batch"]

    def mod_map(i):
        return (jnp.minimum(i // lat_tiles, nb), 0, 0)

    def kv_pos(i):
        lat_b = i // lat_tiles
        lat_pos = lat_b * kv_tiles + ctx_tiles + i % lat_tiles
        j = i - nb * lat_tiles
        ctx_pos = (j // ctx_tiles) * kv_tiles + j % ctx_tiles
        return jnp.where(i < nb * lat_tiles, lat_pos, ctx_pos)

    row = lambda w: pl.BlockSpec((tm, w), lambda i: (i, 0))
    kv_rows = nb * (dims["seq"] + dims["ctx"])
    return pl.pallas_call(
        _proj_in_kernel,
        grid=(nt,),
        in_specs=[
            row(D_MODEL),
            pl.BlockSpec((1, 6, D_MODEL), mod_map),
            row(LANE), row(LANE),
            _resident((D_MODEL, P_WIDTH)),
            _resident((1, GM_WIDTH)), _resident((1, GM_WIDTH)),
            _resident((GM_HEADS, GM_CHUNK, GM_CHUNK)), _resident((GM_CHUNK, GM_WIDTH)),
            _resident((1, Q_LORA)), _resident((Q_LORA, MLA_HEADS * HEAD_PAD)),
            _resident((1, KV_LORA)), _resident((KV_LORA, MLA_HEADS * QK_NOPE)),
            _resident((MLA_HEADS * V_DIM, KV_LORA)),
        ],
        out_specs=[
            row(GM_WIDTH), row(SSM_INNER), row(XBC_WIDTH), row(2 * LANE),
            row(MLA_HEADS * HEAD_PAD),
            pl.BlockSpec((tm, MLA_HEADS * HEAD_PAD), lambda i: (kv_pos(i), 0)),
            pl.BlockSpec((MLA_HEADS * V_DIM, tm), lambda i: (0, kv_pos(i))),
        ],
        out_shape=[
            jax.ShapeDtypeStruct((rows, GM_WIDTH), BF16),
            jax.ShapeDtypeStruct((rows, SSM_INNER), F32),
            jax.ShapeDtypeStruct((rows, XBC_WIDTH), F32),
            jax.ShapeDtypeStruct((rows, 2 * LANE), F32),
            jax.ShapeDtypeStruct((rows, MLA_HEADS * HEAD_PAD), BF16),
            jax.ShapeDtypeStruct((kv_rows, MLA_HEADS * HEAD_PAD), BF16),
            jax.ShapeDtypeStruct((MLA_HEADS * V_DIM, kv_rows), BF16),
        ],
        compiler_params=_cparams(("arbitrary",)),
        name="proj_in",
    )(x_all, modv, ca, cb, wp["w_in"], wp["g_v"], wp["b_v"], wp["w_sp"], wp["b_sp"],
      wp["g_q"], wp["w_q"], wp["g_kv"], wp["w_k"], wp["w_vt"])


def _ssd_kernel(backward, n_ctx, n_lat, *refs):
    if backward:
        (xp_ref, xc_ref, xn_ref, dt_ref, cw_ref, cbias_ref, dtb_ref, alog_ref, e_ref,
         yf_ref, z_ref, skip_ref, gssm_ref, out_ref, h_scr) = refs
    else:
        (xp_ref, xc_ref, xn_ref, dt_ref, cw_ref, cbias_ref, dtb_ref, alog_ref, e_ref,
         out_ref, h_scr) = refs
    q = SSM_CHUNK
    rows = SSD_STEP * q
    s = pl.program_id(1)
    n_ctx_s = n_ctx // SSD_STEP
    n_lat_s = n_lat // SSD_STEP
    is_ctx = s < n_ctx_s
    if backward:
        cc = jnp.where(is_ctx, n_ctx_s - 1 - s, n_lat_s - 1 - (s - n_ctx_s))
    else:
        cc = jnp.where(is_ctx, s, s - n_ctx_s)
    ncc = jnp.where(is_ctx, n_ctx_s, n_lat_s)
    at_start = cc == 0
    at_end = cc == ncc - 1

    @pl.when(s == 0)
    def _():
        h_scr[...] = jnp.zeros_like(h_scr)

    prev = jnp.where(at_start, 0.0, xp_ref[...])
    nxt = jnp.where(at_end, 0.0, xn_ref[...])
    ext = jnp.concatenate([prev, xc_ref[...], nxt], axis=0)
    acc = jnp.broadcast_to(cbias_ref[...], (rows, XBC_WIDTH))
    base = 8 - SSM_CONV // 2
    for j in range(SSM_CONV):
        acc = acc + cw_ref[j:j + 1, :] * ext[base + j:base + j + rows, :]
    u_all = _silu(acc)

    lane = lax.broadcasted_iota(jnp.int32, (rows, LANE), 1)
    dtr = dt_ref[...] + dtb_ref[...]
    dtv_all = jnp.maximum(dtr, 0.0) + jnp.log1p(jnp.exp(-jnp.abs(dtr)))
    dtv_all = jnp.where(lane < SSM_HEADS, dtv_all, 0.0)
    da_all = dtv_all * (-jnp.exp(alog_ref[...]))

    ri = lax.broadcasted_iota(jnp.int32, (q, q), 0)
    ci = lax.broadcasted_iota(jnp.int32, (q, q), 1)
    mask = (ci >= ri) if backward else (ci <= ri)
    tri = jnp.where(mask, 1.0, 0.0).astype(BF16)
    lane_w = lax.broadcasted_iota(jnp.int32, (q, SSM_INNER), 1) // SSM_HEAD_DIM
    last = 0 if backward else q - 1
    per_g = SSM_HEADS // SSM_GROUPS
    gw = per_g * SSM_HEAD_DIM

    h_cur = h_scr[...]
    for sub in (range(SSD_STEP - 1, -1, -1) if backward else range(SSD_STEP)):
        r0 = sub * q
        u = u_all[r0:r0 + q, :]
        xc = u[:, :SSM_INNER]
        dtv = dtv_all[r0:r0 + q, :]
        acs = _dot_01(tri, da_all[r0:r0 + q, :])
        acs_t = acs.T
        acs_last = acs[last:last + 1, :]
        small = jnp.concatenate([dtv, jnp.exp(acs_last - acs), jnp.exp(acs)], axis=0)
        wide = _dot_01(small, e_ref[...], ones_on_right=True)
        dt_w = wide[:q]
        decay_w = wide[q:2 * q]
        eacs_w = wide[2 * q:]
        elast_w = eacs_w[last:last + 1, :]

        xdt = xc * dt_w
        xw_b = (xdt * decay_w).astype(BF16)
        h_b = h_cur.astype(BF16)

        w_parts = []
        x_parts = []
        inter = []
        upd = []
        for g in range(SSM_GROUPS):
            b_g = u[:, SSM_INNER + g * SSM_STATE:SSM_INNER + (g + 1) * SSM_STATE].astype(BF16)
            o_c = SSM_INNER + SSM_GROUPS * SSM_STATE
            c_g = u[:, o_c + g * SSM_STATE:o_c + (g + 1) * SSM_STATE].astype(BF16)
            cbm = lax.dot_general(c_g, b_g, (((1,), (1,)), ((), ())), preferred_element_type=F32)
            for hh in range(per_g):
                hd = g * per_g + hh
                seg = acs[:, hd:hd + 1] - acs_t[hd:hd + 1, :]
                lmat = jnp.exp(jnp.where(mask, seg, -jnp.inf))
                w_parts.append((cbm * lmat).astype(BF16))
                x_parts.append(jnp.where(lane_w == hd, xdt, 0.0).astype(BF16))
            inter.append(jnp.dot(c_g, h_b[:, g * gw:(g + 1) * gw], preferred_element_type=F32))
            upd.append(lax.dot_general(b_g, xw_b[:, g * gw:(g + 1) * gw], (((0,), (0,)), ((), ())),
                                       preferred_element_type=F32))
        y = jnp.dot(jnp.concatenate(w_parts, axis=1), jnp.concatenate(x_parts, axis=0),
                    preferred_element_type=F32)
        y = y + jnp.concatenate(inter, axis=1) * eacs_w
        h_cur = h_cur * elast_w + jnp.concatenate(upd, axis=1)

        if backward:
            tot = yf_ref[r0:r0 + q, :] + y + skip_ref[...] * xc
            yz = tot * _silu(z_ref[r0:r0 + q, :])
            half = SSM_INNER // SSM_GROUPS
            outs = []
            for g in range(SSM_GROUPS):
                grp = yz[:, g * half:(g + 1) * half]
                outs.append(grp * lax.rsqrt(jnp.mean(grp * grp, axis=-1, keepdims=True) + RMS_EPS))
            out_ref[r0:r0 + q, :] = (jnp.concatenate(outs, axis=1) * gssm_ref[...]).astype(BF16)
        else:
            out_ref[r0:r0 + q, :] = y
    h_scr[...] = h_cur


def _ssd_sweep(backward, xbc, dtc, wp, dims, yf=None, z=None):
    nb = dims["batch"]
    n_lat = dims["seq"] // SSM_CHUNK
    n_ctx = dims["ctx"] // SSM_CHUNK
    assert n_lat % SSD_STEP == 0 and n_ctx % SSD_STEP == 0
    n_lat_s = n_lat // SSD_STEP
    n_ctx_s = n_ctx // SSD_STEP
    ctx_base = nb * n_lat_s
    rows = xbc.shape[0]
    step_rows = SSD_STEP * SSM_CHUNK
    sub = step_rows // 8

    def blk(b, s):
        is_ctx = s < n_ctx_s
        if backward:
            cc = jnp.where(is_ctx, n_ctx_s - 1 - s, n_lat_s - 1 - (s - n_ctx_s))
        else:
            cc = jnp.where(is_ctx, s, s - n_ctx_s)
        return jnp.where(is_ctx, ctx_base + b * n_ctx_s + cc, b * n_lat_s + cc)

    cur = lambda w: pl.BlockSpec((step_rows, w), lambda b, s: (blk(b, s), 0))
    d = 1 if backward else 0
    in_specs = [
        pl.BlockSpec((8, XBC_WIDTH), lambda b, s: (jnp.maximum(blk(b, s) * sub - 1, 0), 0)),
        cur(XBC_WIDTH),
        pl.BlockSpec((8, XBC_WIDTH), lambda b, s: (jnp.minimum((blk(b, s) + 1) * sub, rows // 8 - 1), 0)),
        pl.BlockSpec((step_rows, LANE), lambda b, s: (blk(b, s), d)),
        _resident((8, XBC_WIDTH)), _resident((1, XBC_WIDTH)),
        _resident((1, LANE)), _resident((1, LANE)), _resident((LANE, SSM_INNER)),
    ]
    args = [xbc, xbc, xbc, dtc, wp["conv_w"], wp["conv_b"], wp["dt_bias"][d], wp["a_log"][d], wp["expand"]]
    if backward:
        in_specs += [cur(SSM_INNER), cur(SSM_INNER), _resident((1, SSM_INNER)), _resident((1, SSM_INNER))]
        args += [yf, z, wp["d_skip"], wp["g_ssm"]]
    return pl.pallas_call(
        functools.partial(_ssd_kernel, backward, n_ctx, n_lat),
        grid=(nb, n_ctx_s + n_lat_s),
        in_specs=in_specs,
        out_specs=cur(SSM_INNER),
        out_shape=jax.ShapeDtypeStruct((rows, SSM_INNER), BF16 if backward else F32),
        scratch_shapes=[pltpu.VMEM((SSM_STATE, SSM_INNER), F32)],
        compiler_params=_cparams(("arbitrary", "arbitrary")),
        name="ssd_bwd" if backward else "ssd_fwd",
    )(*args)


def _attn_kernel(nk, tk, q_ref, k_ref, vt_ref, o_ref):
    q = q_ref[...]
    tq = q.shape[0]
    m = jnp.full((1, tq), -jnp.inf, F32)
    l = jnp.zeros((1, tq), F32)
    acc = jnp.zeros((V_DIM, tq), F32)
    for i in range(nk):
        k = k_ref[i * tk:(i + 1) * tk, :]
        vt = vt_ref[:, i * tk:(i + 1) * tk]
        st = lax.dot_general(k, q, (((1,), (1,)), ((), ())), preferred_element_type=F32)
        m_new = jnp.maximum(m, jnp.max(st, axis=0, keepdims=True))
        alpha = jnp.exp2(m - m_new)
        p = jnp.exp2(st - m_new)
        l = alpha * l + jnp.sum(p, axis=0, keepdims=True)
        acc = alpha * acc + jnp.dot(vt, p.astype(BF16), preferred_element_type=F32)
        m = m_new
    o_ref[...] = (acc / l).T.astype(BF16)


def _attention(q, k, vt, nb, lq, lk, q_row0, kv_stride, tq, tk):
    nq = lq // tq
    q0 = q_row0 // tq
    assert kv_stride % lk == 0 and lk % tk == 0 and q_row0 % tq == 0
    kvb = kv_stride // lk
    return pl.pallas_call(
        functools.partial(_attn_kernel, lk // tk, tk),
        grid=(nb, MLA_HEADS, nq),
        in_specs=[
            pl.BlockSpec((tq, HEAD_PAD), lambda b, h, i: (q0 + b * nq + i, h)),
            pl.BlockSpec((lk, HEAD_PAD), lambda b, h, i: (b * kvb, h)),
            pl.BlockSpec((V_DIM, lk), lambda b, h, i: (h, b * kvb)),
        ],
        out_specs=pl.BlockSpec((tq, V_DIM), lambda b, h, i: (b * nq + i, h)),
        out_shape=jax.ShapeDtypeStruct((nb * lq, MLA_HEADS * V_DIM), BF16),
        compiler_params=_cparams(("arbitrary", "arbitrary", "arbitrary")),
        name="mla_attention",
    )(q, k, vt)


def _out_proj_kernel(n_lat_tiles, gm_ref, ssm_ref, att_ref, attc_ref, x_ref, mod_ref, w_ref, g_ref, b_ref, wr_ref,
                     x1_ref, tok_ref, lg_ref):
    mod = mod_ref[0]
    for r0 in range(0, x_ref.shape[0], ROW_TILE):
        rows = slice(r0, r0 + ROW_TILE)
        att = att_ref[rows, :]
        if attc_ref is not None:
            att = jnp.where(pl.program_id(0) >= n_lat_tiles, attc_ref[rows, :], att)
        mixed = jnp.concatenate([gm_ref[rows, :], ssm_ref[rows, :], att], axis=1)
        mix = jnp.dot(mixed, w_ref[...], preferred_element_type=F32)
        x1 = _layer_norm(DN_ALPHA * x_ref[rows, :] + mod[2:3, :] * mix, g_ref[...], b_ref[...])
        x1_ref[rows, :] = x1
        tok = x1 * (1.0 + mod[4:5, :]) + mod[3:4, :]
        tok_ref[rows, :] = _pack_rows(tok)
        lg_ref[:, rows] = lax.dot_general(wr_ref[...], tok, (((1,), (1,)), ((), ())),
                                          precision=HIGHEST, preferred_element_type=F32)


def _out_proj_lat_kernel(n_lat_tiles, gm_ref, ssm_ref, att_ref, *rest):
    _out_proj_kernel(n_lat_tiles, gm_ref, ssm_ref, att_ref, None, *rest)


def _out_proj(gm, ssm, att, att_ctx, x_all, modv, wp, dims, rows):
    tm = OUT_TILE
    assert dims["seq"] % tm == 0 and (dims["batch"] * dims["ctx"]) % tm == 0
    lat_tiles = dims["seq"] // tm
    nb = dims["batch"]
    n_lat = nb * lat_tiles
    row = lambda w: pl.BlockSpec((tm, w), lambda i: (i, 0))
    att_specs = [pl.BlockSpec((tm, MLA_HEADS * V_DIM), lambda i: (jnp.minimum(i, n_lat - 1), 0))]
    att_args = [att]
    body = _out_proj_lat_kernel
    if att_ctx is not None:
        att_specs.append(pl.BlockSpec((tm, MLA_HEADS * V_DIM), lambda i: (jnp.maximum(i - n_lat, 0), 0)))
        att_args.append(att_ctx)
        body = _out_proj_kernel
    return pl.pallas_call(
        functools.partial(body, n_lat),
        grid=(rows // tm,),
        in_specs=[
            row(GM_WIDTH), row(SSM_INNER), *att_specs, row(D_MODEL),
            pl.BlockSpec((1, 6, D_MODEL), lambda i: (jnp.minimum(i // lat_tiles, nb), 0, 0)),
            _resident((D_MODEL, D_MODEL)), _resident((1, D_MODEL)), _resident((1, D_MODEL)),
            _resident((N_EXPERTS, D_MODEL)),
        ],
        out_specs=[row(D_MODEL), row(D_MODEL // 2), pl.BlockSpec((N_EXPERTS, tm), lambda i: (0, i))],
        out_shape=[
            jax.ShapeDtypeStruct((rows, D_MODEL), F32),
            jax.ShapeDtypeStruct((rows, D_MODEL // 2), jnp.uint32),
            jax.ShapeDtypeStruct((N_EXPERTS, rows), F32),
        ],
        compiler_params=_cparams(("arbitrary",)),
        name="out_proj",
    )(gm, ssm, *att_args, x_all, modv, wp["w_out"], wp["ln1_g"], wp["ln1_b"], wp["w_router_t"])


def _gate_kernel(lg_ref, b_ref, eidx_ref, w_ref, pos_ref, cnt_ref, carry_scr):
    per = N_EXPERTS // N_EXPERT_GROUPS
    tn = lg_ref.shape[1]
    sc = jax.nn.sigmoid(lg_ref[...])
    bi = sc + b_ref[...]
    neg = np.float32(-np.inf)
    xs = [bi[j * N_EXPERT_GROUPS:(j + 1) * N_EXPERT_GROUPS, :] for j in range(per)]
    ss = [sc[j * N_EXPERT_GROUPS:(j + 1) * N_EXPERT_GROUPS, :] for j in range(per)]
    m1 = functools.reduce(jnp.maximum, xs)
    first = functools.reduce(jnp.minimum, [jnp.where(xs[j] == m1, j, per) for j in range(per)])
    m2 = functools.reduce(jnp.maximum, [jnp.where(first == j, neg, xs[j]) for j in range(per)])
    gscore = m1 + m2
    gi = lax.broadcasted_iota(jnp.int32, (N_EXPERT_GROUPS, tn), 0)
    cur = gscore
    gsel = jnp.zeros((N_EXPERT_GROUPS, tn), jnp.int32)
    for _ in range(TOPK_GROUPS):
        mx = jnp.max(cur, axis=0, keepdims=True)
        fi = jnp.min(jnp.where(cur == mx, gi, N_EXPERT_GROUPS), axis=0, keepdims=True)
        pick = gi == fi
        gsel = jnp.where(pick, 1, gsel)
        cur = jnp.where(pick, neg, cur)
    cur = [jnp.where(gsel > 0, xs[j], neg) for j in range(per)]
    eid = [gi * per + j for j in range(per)]
    sel = [jnp.zeros((N_EXPERT_GROUPS, tn), F32) for _ in range(per)]
    picked = []
    for _ in range(TOP_K):
        mx = jnp.max(functools.reduce(jnp.maximum, cur), axis=0, keepdims=True)
        cand = functools.reduce(jnp.minimum, [jnp.where(cur[j] == mx, eid[j], N_EXPERTS) for j in range(per)])
        fi = jnp.min(cand, axis=0, keepdims=True)
        wk = jnp.zeros((N_EXPERT_GROUPS, tn), F32)
        for j in range(per):
            pick = eid[j] == fi
            sel[j] = jnp.where(pick, 1.0, sel[j])
            cur[j] = jnp.where(pick, neg, cur[j])
            wk = wk + jnp.where(pick, ss[j], 0.0)
        picked.append((fi, jnp.sum(wk, axis=0, keepdims=True)))
    tot = functools.reduce(jnp.add, [wk for _, wk in picked])

    @pl.when(pl.program_id(0) == 0)
    def _():
        carry_scr[...] = jnp.zeros_like(carry_scr)

    sel_all = jnp.concatenate(sel, axis=0)
    ri = lax.broadcasted_iota(jnp.int32, (tn, tn), 0)
    ci = lax.broadcasted_iota(jnp.int32, (tn, tn), 1)
    before = jnp.where(ri < ci, 1.0, 0.0).astype(BF16)
    prefix = jnp.dot(sel_all.astype(BF16), before, preferred_element_type=F32)
    carry = carry_scr[...]
    rank = prefix + carry
    carry_new = carry + prefix[:, tn - 1:tn] + sel_all[:, tn - 1:tn]
    carry_scr[...] = carry_new
    cnt_ref[...] = jnp.broadcast_to(carry_new, cnt_ref.shape).astype(jnp.int32)
    for k, (fi, wk) in enumerate(picked):
        pk = jnp.zeros((N_EXPERT_GROUPS, tn), F32)
        for j in range(per):
            pk = pk + jnp.where(eid[j] == fi, rank[j * N_EXPERT_GROUPS:(j + 1) * N_EXPERT_GROUPS, :], 0.0)
        eidx_ref[k:k + 1, :] = fi
        w_ref[k:k + 1, :] = wk / tot * ROUTED_SCALE
        pos_ref[k:k + 1, :] = jnp.sum(pk, axis=0, keepdims=True).astype(jnp.int32)


def _gate(logits_t, b_router_col):
    t = logits_t.shape[1]
    tn = 512
    tile = pl.BlockSpec((TOP_K, tn), lambda i: (0, i))
    return pl.pallas_call(
        _gate_kernel,
        grid=(t // tn,),
        in_specs=[pl.BlockSpec((N_EXPERTS, tn), lambda i: (0, i)), _resident((N_EXPERTS, 1))],
        out_specs=[tile, tile, tile, pl.BlockSpec((N_EXPERTS, LANE), lambda i: (0, 0))],
        out_shape=[
            jax.ShapeDtypeStruct((TOP_K, t), jnp.int32),
            jax.ShapeDtypeStruct((TOP_K, t), F32),
            jax.ShapeDtypeStruct((TOP_K, t), jnp.int32),
            jax.ShapeDtypeStruct((N_EXPERTS, LANE), jnp.int32),
        ],
        scratch_shapes=[pltpu.VMEM((N_EXPERTS, 1), F32)],
        compiler_params=_cparams(("arbitrary",)),
        name="gate_topk",
    )(logits_t, b_router_col)


DISPATCH_TILE = 512


def _dispatch_kernel(padlo_ref, padn_ref, nu_ref, slot_ref, tok_ref, xs_hbm, zero_buf, sem, zsem):
    i = pl.program_id(0)
    td = slot_ref.shape[1]

    def row_copy(r, k):
        return pltpu.make_async_copy(tok_ref.at[pl.ds(r, 1)], xs_hbm.at[pl.ds(slot_ref[k, r], 1)], sem)

    def issue(r, carry):
        for k in range(TOP_K):
            row_copy(r, k).start(priority=k % 2)
        return carry

    lax.fori_loop(0, td, issue, 0)

    def zero_fill(start):
        def go(cp):
            if start:
                cp.start()
            else:
                cp.wait()

        def fill(e, carry):
            lo = padlo_ref[e]
            n = padn_ref[e]
            head = jnp.minimum((-lo) & 7, n)

            for s in range(7):
                @pl.when(s < head)
                def _(s=s):
                    go(pltpu.make_async_copy(zero_buf.at[pl.ds(0, 1)], xs_hbm.at[pl.ds(lo + s, 1)], zsem))

            rest = n - head
            off = lo + head
            size = EXPERT_BLOCK // 2
            while size >= 8:
                hit = (rest & size) != 0

                @pl.when(hit)
                def _(off=off, size=size):
                    dst = xs_hbm.at[pl.ds(pl.multiple_of(off, 8), size)]
                    go(pltpu.make_async_copy(zero_buf.at[pl.ds(0, size)], dst, zsem))

                off = off + jnp.where(hit, size, 0)
                size //= 2
            return carry

        lax.fori_loop(0, N_EXPERTS, fill, 0)
        zrows = zero_buf.shape[0]

        def fill_tail(b, carry):
            for h in range(EXPERT_BLOCK // zrows):
                row0 = pl.multiple_of(b * EXPERT_BLOCK + h * zrows, 8)
                go(pltpu.make_async_copy(zero_buf, xs_hbm.at[pl.ds(row0, zrows)], zsem))
            return carry

        lax.fori_loop(nu_ref[0], xs_hbm.shape[0] // EXPERT_BLOCK, fill_tail, 0)

    @pl.when(i == 0)
    def _():
        zero_buf[...] = jnp.zeros_like(zero_buf)
        zero_fill(True)

    for _ in range(TOP_K):
        pltpu.make_async_copy(tok_ref, xs_hbm.at[pl.ds(0, td)], sem).wait()

    @pl.when(i == 0)
    def _():
        zero_fill(False)


def _dispatch_rows(slot, pad_lo, pad_n, n_used, tok, cap):
    t = tok.shape[0]
    td = DISPATCH_TILE
    return pl.pallas_call(
        _dispatch_kernel,
        grid_spec=pltpu.PrefetchScalarGridSpec(
            num_scalar_prefetch=3,
            grid=(t // td,),
            in_specs=[
                pl.BlockSpec((TOP_K, td), lambda i, lo, n, nu: (0, i), memory_space=pltpu.SMEM),
                pl.BlockSpec((td, D_MODEL // 2), lambda i, lo, n, nu: (i, 0)),
            ],
            out_specs=pl.BlockSpec(memory_space=pl.ANY),
            scratch_shapes=[pltpu.VMEM((EXPERT_BLOCK // 2, D_MODEL // 2), jnp.uint32),
                            pltpu.SemaphoreType.DMA, pltpu.SemaphoreType.DMA],
        ),
        out_shape=jax.ShapeDtypeStruct((cap, D_MODEL // 2), jnp.uint32),
        compiler_params=_cparams(("arbitrary",)),
        name="dispatch_rows",
    )(pad_lo, pad_n, n_used, slot, tok)


def _experts_kernel(be_ref, nu_ref, x_ref, wg_ref, wu_ref, wd_ref, y_ref, wg_s, wu_s, wd_s):
    j = pl.program_id(0)
    prev = be_ref[jnp.maximum(j - 1, 0)]
    fresh = jnp.logical_or(j == 0, be_ref[j] != prev)

    @pl.when(fresh)
    def _():
        wg_s[...] = wg_ref[0].astype(BF16)
        wu_s[...] = wu_ref[0].astype(BF16)
        wd_s[...] = wd_ref[0].astype(BF16)

    @pl.when(j < nu_ref[0])
    def _():
        x = _unpack_rows(x_ref[...])
        g = jnp.dot(x, wg_s[...], preferred_element_type=F32)
        u = jnp.dot(x, wu_s[...], preferred_element_type=F32)
        hb = (_silu(g) * u).astype(BF16)
        y_ref[...] = _pack_rows(jnp.dot(hb, wd_s[...], preferred_element_type=F32))

    @pl.when(j >= nu_ref[0])
    def _():
        y_ref[...] = jnp.zeros_like(y_ref)


def _experts(layer, block_e, n_used, x_sorted, w_gate, w_up, w_down):
    cap = x_sorted.shape[0]
    blk = EXPERT_BLOCK
    return pl.pallas_call(
        _experts_kernel,
        grid_spec=pltpu.PrefetchScalarGridSpec(
            num_scalar_prefetch=2,
            grid=(cap // blk,),
            in_specs=[
                pl.BlockSpec((blk, D_MODEL // 2), lambda j, be, nu: (jnp.minimum(j, nu[0] - 1), 0)),
                pl.BlockSpec((None, 1, D_MODEL, EXPERT_FF), lambda j, be, nu: (layer, be[j], 0, 0)),
                pl.BlockSpec((None, 1, D_MODEL, EXPERT_FF), lambda j, be, nu: (layer, be[j], 0, 0)),
                pl.BlockSpec((None, 1, EXPERT_FF, D_MODEL), lambda j, be, nu: (layer, be[j], 0, 0)),
            ],
            out_specs=pl.BlockSpec((blk, D_MODEL // 2), lambda j, be, nu: (j, 0)),
            scratch_shapes=[pltpu.VMEM((D_MODEL, EXPERT_FF), BF16), pltpu.VMEM((D_MODEL, EXPERT_FF), BF16),
                            pltpu.VMEM((EXPERT_FF, D_MODEL), BF16)],
        ),
        out_shape=jax.ShapeDtypeStruct((cap, D_MODEL // 2), jnp.uint32),
        compiler_params=_cparams(("arbitrary",)),
        name="routed_experts",
    )(block_e, n_used, x_sorted, w_gate, w_up, w_down)


COMBINE_TILE = 256


def _final_kernel(slot_ref, slot_next_ref, w8_ref, tok_ref, x1_ref, mod_ref, wgu_ref, wd_ref, g_ref, b_ref,
                  y_hbm, o_ref, ybuf, sems):
    i = pl.program_id(0)
    n = pl.num_programs(0)
    tf = tok_ref.shape[0]
    cur = i % 2

    def gather(slots, buf):
        def issue(r, carry):
            for k in range(TOP_K):
                pltpu.make_async_copy(y_hbm.at[pl.ds(slots[k, r], 1)], ybuf.at[buf, k, pl.ds(r, 1)],
                                      sems.at[buf]).start(priority=k % 2)
            return carry

        lax.fori_loop(0, tf, issue, 0)

    @pl.when(i == 0)
    def _():
        gather(slot_ref, 0)

    @pl.when(i + 1 < n)
    def _():
        gather(slot_next_ref, 1 - cur)

    mod = mod_ref[0]
    gu = jnp.dot(_unpack_rows(tok_ref[...]), wgu_ref[...], preferred_element_type=F32)
    hb = (_silu(gu[:, :SHARED_FF]) * gu[:, SHARED_FF:]).astype(BF16)
    f = jnp.dot(hb, wd_ref[...], preferred_element_type=F32)

    for k in range(TOP_K):
        pltpu.make_async_copy(y_hbm.at[pl.ds(0, tf)], ybuf.at[cur, k], sems.at[cur]).wait()
    w8 = w8_ref[...]
    half = D_MODEL // 2
    f_lo = f[:, :half]
    f_hi = f[:, half:]
    for k in range(TOP_K):
        yk = ybuf[cur, k]
        wk = w8[:, k:k + 1]
        f_lo = f_lo + wk * pltpu.bitcast(yk << 16, F32)
        f_hi = f_hi + wk * pltpu.bitcast(yk & jnp.uint32(0xFFFF0000), F32)
    f = jnp.concatenate([f_lo, f_hi], axis=1)
    o_ref[...] = _layer_norm(DN_ALPHA * x1_ref[...] + mod[5:6, :] * f, g_ref[...], b_ref[...])


def _final(slot, w8_t, tok, y_sorted, x1, modv, wp, dims, rows):
    tf = COMBINE_TILE
    nt = rows // tf
    lat_tiles = dims["seq"] // tf
    nb = dims["batch"]
    row = lambda w: pl.BlockSpec((tf, w), lambda i: (i, 0))
    return pl.pallas_call(
        _final_kernel,
        grid=(nt,),
        in_specs=[
            pl.BlockSpec((TOP_K, tf), lambda i: (0, i), memory_space=pltpu.SMEM),
            pl.BlockSpec((TOP_K, tf), lambda i: (0, jnp.minimum(i + 1, nt - 1)), memory_space=pltpu.SMEM),
            row(TOP_K), row(D_MODEL // 2), row(D_MODEL),
            pl.BlockSpec((1, 6, D_MODEL), lambda i: (jnp.minimum(i // lat_tiles, nb), 0, 0)),
            _resident((D_MODEL, 2 * SHARED_FF)), _resident((SHARED_FF, D_MODEL)),
            _resident((1, D_MODEL)), _resident((1, D_MODEL)),
            pl.BlockSpec(memory_space=pl.ANY),
        ],
        out_specs=row(D_MODEL),
        out_shape=jax.ShapeDtypeStruct((rows, D_MODEL), F32),
        scratch_shapes=[pltpu.VMEM((2, TOP_K, tf, D_MODEL // 2), jnp.uint32), pltpu.SemaphoreType.DMA((2,))],
        compiler_params=_cparams(("arbitrary",)),
        name="combine_shared_ln2",
    )(slot, slot, w8_t, tok, x1, modv, wp["w_sh_gu"], wp["w_sh_down"], wp["ln2_g"], wp["ln2_b"], y_sorted)


def _rot_half_cols():
    src = np.zeros((QK_ROPE,), np.int32)
    sign = np.zeros((QK_ROPE,), np.float32)
    quarter = QK_ROPE // 4
    for dcol in range(QK_ROPE):
        part, i = divmod(dcol, 2 * quarter)
        half, kk = divmod(i, quarter)
        src[dcol] = part * 2 * quarter + (quarter + kk if half == 0 else kk)
        sign[dcol] = -1.0 if half == 0 else 1.0
    return src, sign


def _expert_major(a, axis):
    per = N_EXPERTS // N_EXPERT_GROUPS
    shp = a.shape
    a = a.reshape(shp[:axis] + (N_EXPERT_GROUPS, per) + shp[axis + 1:])
    a = jnp.swapaxes(a, axis, axis + 1)
    return a.reshape(shp)


def _prep_layer(l, p):
    src, sign = _rot_half_cols()
    w_in = p["w_in"][l]
    col = lambda lo, n: w_in[:, lo:lo + n]
    zpad = jnp.zeros((D_MODEL, LANE - SSM_HEADS), F32)
    w_kr = col(R_KR, QK_ROPE)
    w_in_p = jnp.concatenate([
        col(R_GM, 2 * GM_WIDTH), col(R_Q, Q_LORA), col(R_Z, SSM_INNER), col(R_XBC, XBC_WIDTH),
        col(R_DT, SSM_HEADS), zpad, col(R_DT + SSM_HEADS, SSM_HEADS), zpad,
        col(R_KV, KV_LORA), w_kr, w_kr[:, src] * sign,
    ], axis=1).astype(BF16)
    wq = p["w_q_b"][l].reshape(Q_LORA, MLA_HEADS, QK_NOPE + QK_ROPE)
    wq_rope = wq[:, :, QK_NOPE:]
    w_q = jnp.concatenate([wq[:, :, :QK_NOPE], wq_rope, wq_rope[:, :, src] * sign], axis=2)
    w_q = w_q.reshape(Q_LORA, MLA_HEADS * HEAD_PAD).astype(BF16)
    wkv = p["w_kv_b"][l].reshape(KV_LORA, MLA_HEADS, QK_NOPE + V_DIM)
    w_k = wkv[:, :, :QK_NOPE].reshape(KV_LORA, -1).astype(BF16)
    w_vt = wkv[:, :, QK_NOPE:].reshape(KV_LORA, -1).T.astype(BF16)
    lane_pad = lambda a: jnp.pad(a, ((0, 0), (0, LANE - a.shape[1])))
    expand = jnp.repeat(jnp.eye(LANE, SSM_HEADS, dtype=BF16), SSM_HEAD_DIM, axis=1)
    r2 = lambda a: a.reshape(1, -1)
    return dict(
        w_in=w_in_p, g_v=r2(p["g_v"][l]), b_v=r2(p["b_v"][l]), w_sp=p["w_sp"][l].astype(BF16),
        b_sp=jnp.repeat(p["b_sp"][l].T, GM_CHUNK, axis=1),
        g_q=r2(p["g_q"][l]), w_q=w_q, g_kv=r2(p["g_kv"][l]), w_k=w_k, w_vt=w_vt,
        conv_w=jnp.pad(p["conv_w"][l], ((0, 8 - SSM_CONV), (0, 0))), conv_b=r2(p["conv_b"][l]),
        dt_bias=lane_pad(p["dt_bias"][l])[:, None, :], a_log=lane_pad(p["a_log"][l])[:, None, :],
        expand=expand, d_skip=r2(jnp.repeat(p["d_skip"][l], SSM_HEAD_DIM)), g_ssm=r2(p["g_ssm"][l]),
        w_out=p["w_out"][l].astype(BF16), ln1_g=r2(p["ln1_g"][l]), ln1_b=r2(p["ln1_b"][l]),
        ln2_g=r2(p["ln2_g"][l]), ln2_b=r2(p["ln2_b"][l]),
        w_router_t=_expert_major(p["w_router"][l].T, 0),
        b_router=_expert_major(p["b_router"][l], 0).reshape(N_EXPERTS, 1),
        w_sh_gu=jnp.concatenate([p["w_sh_gate"][l], p["w_sh_up"][l]], axis=1).astype(BF16),
        w_sh_down=p["w_sh_down"][l].astype(BF16),
    )


def _rope_tables(nb, seq, nctx):
    t = jnp.arange(seq)
    half = QK_ROPE // 2
    inv = 1.0 / (ROPE_BASE ** (jnp.arange(0, half, 2, dtype=F32) / half))
    ang_r = (t // GRID_W)[:, None] * inv
    ang_c = (t % GRID_W)[:, None] * inv
    ang = jnp.concatenate([ang_r, ang_r, ang_c, ang_c], axis=1)
    zero = jnp.zeros((seq, LANE - QK_ROPE), F32)
    ca_lat = jnp.concatenate([jnp.cos(ang), zero], axis=1)
    cb_lat = jnp.concatenate([jnp.sin(ang), zero], axis=1)
    ca_ctx = jnp.concatenate([jnp.ones((nb * nctx, QK_ROPE), F32), jnp.zeros((nb * nctx, LANE - QK_ROPE), F32)], axis=1)
    ca = jnp.concatenate([jnp.tile(ca_lat, (nb, 1)), ca_ctx], axis=0)
    cb = jnp.concatenate([jnp.tile(cb_lat, (nb, 1)), jnp.zeros((nb * nctx, LANE), F32)], axis=0)
    return ca, cb


def _slot_layout(eidx, pos, cnt, t):
    per = N_EXPERTS // N_EXPERT_GROUPS
    blk = EXPERT_BLOCK
    counts = cnt[:, 0].reshape(per, N_EXPERT_GROUPS).T.reshape(N_EXPERTS)
    padded = (counts + blk - 1) // blk * blk
    pend = jnp.cumsum(padded)
    pstart = pend - padded
    n_blocks = (t * TOP_K + N_EXPERTS * (blk - 1) + blk - 1) // blk
    block_e = jnp.sum((pend[None, :] <= (jnp.arange(n_blocks) * blk)[:, None]).astype(jnp.int32), axis=1)
    block_e = jnp.minimum(block_e, N_EXPERTS - 1)
    n_used = (pend[-1] // blk).reshape(1)
    hit = eidx[:, :, None] == jnp.arange(N_EXPERTS, dtype=jnp.int32)
    slot = (jnp.sum(jnp.where(hit, pstart, 0), axis=2) + pos).astype(jnp.int32)
    return (slot, block_e.astype(jnp.int32), n_used.astype(jnp.int32), (pstart + counts).astype(jnp.int32),
            (padded - counts).astype(jnp.int32), n_blocks * blk)


def kernel(x, c, ctx, c_ctx, w_mod, b_mod, w_in, g_q, w_q_b, g_kv, w_kv_b, conv_w, conv_b, a_log, dt_bias, d_skip, g_ssm, g_v, b_v, w_sp, b_sp, w_out, ln1_g, ln1_b, ln2_g, ln2_b, w_router, b_router, w_e_gate, w_e_up, w_e_down, w_sh_gate, w_sh_up, w_sh_down):
    p = dict(w_in=w_in, g_q=g_q, w_q_b=w_q_b, g_kv=g_kv, w_kv_b=w_kv_b, conv_w=conv_w, conv_b=conv_b,
             a_log=a_log, dt_bias=dt_bias, d_skip=d_skip, g_ssm=g_ssm, g_v=g_v, b_v=b_v, w_sp=w_sp, b_sp=b_sp,
             w_out=w_out, ln1_g=ln1_g, ln1_b=ln1_b, ln2_g=ln2_g, ln2_b=ln2_b, w_router=w_router,
             b_router=b_router, w_sh_gate=w_sh_gate, w_sh_up=w_sh_up, w_sh_down=w_sh_down)
    nb, seq, d = x.shape
    nctx = ctx.shape[1]
    depth = w_mod.shape[0]
    assert d == D_MODEL and depth == DEPTH and nb + 1 <= 8
    assert seq % ROW_TILE == 0 and nctx % ROW_TILE == 0 and seq % GRID_W == 0
    dims = dict(batch=nb, seq=seq, ctx=nctx)
    lat_rows = nb * seq
    rows_all = lat_rows + nb * nctx
    lk = seq + nctx

    x_all = jnp.concatenate([x.reshape(lat_rows, d), ctx.reshape(nb * nctx, d)], axis=0)
    cvec = jnp.zeros((8, d), F32).at[:nb].set(c).at[nb].set(c_ctx)
    mods = _modulation(cvec, w_mod, b_mod).reshape(depth, 8, 6, d)[:, :nb + 1]
    ca, cb = _rope_tables(nb, seq, nctx)
    tq = min(1024, seq)
    tk = 768 if lk % 768 == 0 else ROW_TILE

    for l in range(depth):
        last = l == depth - 1
        wp = _prep_layer(l, p)
        modv = mods[l]
        gm, z, xbc, dtc, q, k, vt = _proj_in(x_all, modv, ca, cb, wp, dims)
        yf = _ssd_sweep(False, xbc, dtc, wp, dims)
        ssm = _ssd_sweep(True, xbc, dtc, wp, dims, yf=yf, z=z)
        att = _attention(q, k, vt, nb, seq, lk, 0, lk, tq, tk)
        rows = lat_rows if last else rows_all
        att_ctx = None if last else _attention(q, k, vt, nb, nctx, nctx, lat_rows, lk, ROW_TILE, ROW_TILE)
        x1, tok, logits_t = _out_proj(gm, ssm, att, att_ctx, x_all, modv, wp, dims, rows)
        eidx, w8, pos, cnt = _gate(logits_t, wp["b_router"])
        slot, block_e, n_used, pad_lo, pad_n, cap = _slot_layout(eidx, pos, cnt, rows)
        x_sorted = _dispatch_rows(slot, pad_lo, pad_n, n_used, tok, cap)
        y_sorted = _experts(l, block_e, n_used, x_sorted, w_e_gate, w_e_up, w_e_down)
        x_all = _final(slot, w8.T, tok, y_sorted, x1, modv, wp, dims, rows)
    return x_all.reshape(nb, seq, d)
```
